```python
import math
import jax
import jax.numpy as jnp
from jax import lax
import numpy as np

D_MODEL = 2048
BATCH = 8
SEQ = 2048
DEPTH = 4
DEC_BATCH = 8
DEC_SEQ = 16
PAST_LEN = 4096

CHUNK = 64
N_EVEN = (DEPTH + 1) // 2
N_ODD = DEPTH // 2
EPS = 1e-6
NEG_INF = -1e30

MLA_HEADS = 16
QK_NOPE = 128
QK_ROPE = 64
V_DIM = 128
Q_LORA = 512
KV_LORA = 512
ROPE_THETA = 10000.0
Q_BLOCK = 128
SM_SCALE = (QK_NOPE + QK_ROPE) ** -0.5
MLA_WIDTH = MLA_HEADS * V_DIM

POOL_WINDOWS = (2, 4, 8, 16)
POOL_GROUP = 256
POOL_WIDTH = POOL_GROUP * 4
POOL_HIST = 15

EVEN_IN = Q_LORA + KV_LORA + QK_ROPE + MLA_WIDTH + 2 * POOL_WIDTH
EVEN_MIX = MLA_WIDTH + POOL_WIDTH

CMLP_CHUNK = 128
CMLP_GROUPS = 4
CMLP_WIDTH = D_MODEL
CMLP_GROUP_WIDTH = CMLP_WIDTH // CMLP_GROUPS

CONV_WIDTH = 31
CONV_CH = D_MODEL // 2
CONV_HIST = CONV_WIDTH - 1

ODD_IN = 3 * CMLP_WIDTH + 3 * CONV_CH
ODD_MIX = CMLP_WIDTH + CONV_CH

kernel_name = 'streaming_hybrid_mla_pool_gmlp_conv'


def _split(z, sizes):
    outs, o = [], 0
    for s in sizes:
        outs.append(z[..., o:o + s])
        o += s
    return outs


def _rms(x, g):
    xf = x.astype(jnp.float32)
    y = xf * lax.rsqrt(jnp.mean(xf * xf, axis=-1, keepdims=True) + EPS)
    return (y * g.astype(jnp.float32)).astype(x.dtype)


def _ln(x, g, b):
    xf = x.astype(jnp.float32)
    mu = jnp.mean(xf, axis=-1, keepdims=True)
    var = jnp.mean(jnp.square(xf - mu), axis=-1, keepdims=True)
    y = (xf - mu) * lax.rsqrt(var + EPS)
    return (y * g.astype(jnp.float32) + b.astype(jnp.float32)).astype(x.dtype)


def _rope(x, pos):
    half = x.shape[-1] // 2
    inv = jnp.power(jnp.float32(ROPE_THETA), -jnp.arange(half, dtype=jnp.float32) / half)
    ang = pos.astype(jnp.float32)[:, None] * inv[None, :]
    if x.ndim == 4:
        ang = ang[:, None, :]
    cos, sin = jnp.cos(ang), jnp.sin(ang)
    xf = x.astype(jnp.float32)
    x1, x2 = xf[..., :half], xf[..., half:]
    return jnp.concatenate([x1 * cos - x2 * sin, x2 * cos + x1 * sin], axis=-1).astype(x.dtype)


def _chunk_mask(q_pos, k_pos):
    return (k_pos[None, :] // CHUNK) <= (q_pos[:, None] // CHUNK)


def _masked_softmax(s, mask):
    return jax.nn.softmax(jnp.where(mask[None, None], s, NEG_INF), axis=-1)


def _mla_prompt(q_nope, q_rope, c_kv, k_rope, pos, w_uk, w_uv):
    B, S = q_nope.shape[0], q_nope.shape[1]
    k_nope = jnp.einsum('bsc,chd->bshd', c_kv, w_uk)
    v = jnp.einsum('bsc,chd->bshd', c_kv, w_uv)
    outs = []
    for blk in range(S // Q_BLOCK):
        q0, k_end = blk * Q_BLOCK, (blk + 1) * Q_BLOCK
        s = (jnp.einsum('bqhd,bkhd->bhqk', q_nope[:, q0:k_end], k_nope[:, :k_end],
                        preferred_element_type=jnp.float32)
             + jnp.einsum('bqhr,bkr->bhqk', q_rope[:, q0:k_end], k_rope[:, :k_end],
                          preferred_element_type=jnp.float32)) * SM_SCALE
        p = _masked_softmax(s, _chunk_mask(pos[q0:k_end], pos[:k_end]))
        outs.append(jnp.einsum('bhqk,bkhd->bqhd', p.astype(v.dtype), v[:, :k_end]))
    return jnp.concatenate(outs, axis=1)


def _mla_absorbed(q_nope, q_rope, ckv_all, kr_all, q_pos, k_pos, w_uk, w_uv):
    q_lat = jnp.einsum('bqhd,chd->bqhc', q_nope, w_uk)
    s = (jnp.einsum('bqhc,bkc->bhqk', q_lat, ckv_all, preferred_element_type=jnp.float32)
         + jnp.einsum('bqhr,bkr->bhqk', q_rope, kr_all, preferred_element_type=jnp.float32)) * SM_SCALE
    p = _masked_softmax(s, _chunk_mask(q_pos, k_pos))
    o_lat = jnp.einsum('bhqk,bkc->bqhc', p.astype(ckv_all.dtype), ckv_all)
    return jnp.einsum('bqhc,chd->bqhd', o_lat, w_uv)


def _pool_mix(u, hist, pos, w_pool, pool_scale):
    B, T, _ = u.shape
    full_raw = jnp.concatenate([hist, u], axis=1)
    full = full_raw.astype(jnp.float32)
    cs = jnp.concatenate([jnp.zeros((B, 1, POOL_WIDTH), jnp.float32), jnp.cumsum(full, axis=1)], axis=1)
    means = []
    for g, w in enumerate(POOL_WINDOWS):
        sl = slice(g * POOL_GROUP, (g + 1) * POOL_GROUP)
        win_sum = cs[:, POOL_HIST + 1:POOL_HIST + 1 + T, sl] - cs[:, POOL_HIST + 1 - w:POOL_HIST + 1 - w + T, sl]
        n = jnp.minimum(pos + 1, w).astype(jnp.float32)[None, :, None]
        means.append(win_sum / n)
    d = (jnp.concatenate(means, axis=-1) - full[:, POOL_HIST:]).reshape(B, T, 4, POOL_GROUP)
    y = jnp.einsum('btgc,gcd->btgd', d, w_pool.astype(jnp.float32)).reshape(B, T, POOL_WIDTH)
    y = y * pool_scale.astype(jnp.float32)
    return y.astype(u.dtype), full_raw[:, -POOL_HIST:]


def _even_layer(h, pos, past_ckv, past_kr, pool_hist, w_in, g_q, g_kv, w_uq, w_uk, w_uv, w_pool, pool_scale, w_out):
    B, T, _ = h.shape
    z = jnp.einsum('btd,de->bte', h, w_in)
    c_q, c_kv, k_r, g_a, u_b, g_b = _split(z, (Q_LORA, KV_LORA, QK_ROPE, MLA_WIDTH, POOL_WIDTH, POOL_WIDTH))
    q = jnp.einsum('btc,chd->bthd', _rms(c_q, g_q), w_uq)
    q_nope, q_rope = q[..., :QK_NOPE], _rope(q[..., QK_NOPE:], pos)
    c_kv = _rms(c_kv, g_kv)
    k_r = _rope(k_r, pos)
    if past_ckv is None:
        attn = _mla_prompt(q_nope, q_rope, c_kv, k_r, pos, w_uk, w_uv)
    else:
        ckv_all = jnp.concatenate([past_ckv, c_kv], axis=1)
        kr_all = jnp.concatenate([past_kr, k_r], axis=1)
        k_pos = jnp.arange(ckv_all.shape[1])
        attn = _mla_absorbed(q_nope, q_rope, ckv_all, kr_all, pos, k_pos, w_uk, w_uv)
    a = attn.reshape(B, T, MLA_WIDTH) * jax.nn.silu(g_a)
    pooled, new_hist = _pool_mix(u_b, pool_hist, pos, w_pool, pool_scale)
    b = pooled * jax.nn.silu(g_b)
    out = jnp.einsum('bte,ed->btd', jnp.concatenate([a, b], axis=-1), w_out)
    return out, c_kv, k_r, new_hist


def _odd_layer(h, conv_hist, w_in, ln_v_g, ln_v_b, w_sp, b_sp, w_dw, b_dw, ln_c_g, ln_c_b, w_out):
    B, T, _ = h.shape
    z = jnp.einsum('btd,de->bte', h, w_in)
    u, v, g_c, glu_a, glu_b, g_d = _split(z, (CMLP_WIDTH, CMLP_WIDTH, CMLP_WIDTH, CONV_CH, CONV_CH, CONV_CH))
    v = _ln(v, ln_v_g, ln_v_b)
    L = min(T, CMLP_CHUNK)
    tri = jnp.tril(jnp.ones((L, L), dtype=bool))
    ws = jnp.where(tri[None], w_sp[:, :L, :L], 0.0)
    vb = v.reshape(B, T // L, L, CMLP_GROUPS, CMLP_GROUP_WIDTH)
    mixed = jnp.einsum('gij,bnjgc->bnigc', ws, vb) + jnp.transpose(b_sp[:, :L])[None, None, :, :, None]
    c = u * mixed.reshape(B, T, CMLP_WIDTH) * jax.nn.silu(g_c)
    x_c = glu_a * jax.nn.sigmoid(glu_b)
    full = jnp.concatenate([conv_hist, x_c], axis=1)
    y = lax.conv_general_dilated(full, w_dw[:, None, :], window_strides=(1,), padding='VALID',
                                 dimension_numbers=('NWC', 'WIO', 'NWC'),
                                 feature_group_count=CONV_CH) + b_dw
    y = jax.nn.silu(_ln(y, ln_c_g, ln_c_b)) * jax.nn.silu(g_d)
    out = jnp.einsum('bte,ed->btd', jnp.concatenate([c, y], axis=-1), w_out)
    return out, v, full[:, -CONV_HIST:]


def setup_inputs(seed: int = 0) -> dict:
    key = jax.random.key(seed)
    ks = jax.random.split(key, 32)
    f32 = jnp.float32

    def nrm(k, shape, scale):
        return jax.random.normal(k, shape, f32) * scale

    def gain(k, shape):
        return 1.0 + 0.02 * jax.random.normal(k, shape, f32)

    return {
        'x_prompt': nrm(ks[0], (BATCH, SEQ, D_MODEL), 1.0),
        'x_sample': nrm(ks[1], (DEC_BATCH, DEC_SEQ, D_MODEL), 1.0),
        'cache_mla_ckv': nrm(ks[2], (N_EVEN, DEC_BATCH, PAST_LEN, KV_LORA), 1.0),
        'cache_mla_krope': nrm(ks[3], (N_EVEN, DEC_BATCH, PAST_LEN, QK_ROPE), 1.0),
        'state_pool': nrm(ks[4], (N_EVEN, DEC_BATCH, POOL_HIST, POOL_WIDTH), 0.5),
        'state_conv': nrm(ks[5], (N_ODD, DEC_BATCH, CONV_HIST, CONV_CH), 0.5),
        'ln_pre': gain(ks[6], (DEPTH, D_MODEL)),
        'ln_post': gain(ks[7], (DEPTH, D_MODEL)),
        'w_in_even': nrm(ks[8], (N_EVEN, D_MODEL, EVEN_IN), D_MODEL ** -0.5),
        'g_q_lat': gain(ks[9], (N_EVEN, Q_LORA)),
        'g_kv_lat': gain(ks[10], (N_EVEN, KV_LORA)),
        'w_uq': nrm(ks[11], (N_EVEN, Q_LORA, MLA_HEADS, QK_NOPE + QK_ROPE), Q_LORA ** -0.5),
        'w_uk': nrm(ks[12], (N_EVEN, KV_LORA, MLA_HEADS, QK_NOPE), KV_LORA ** -0.5),
        'w_uv': nrm(ks[13], (N_EVEN, KV_LORA, MLA_HEADS, V_DIM), KV_LORA ** -0.5),
        'w_pool': nrm(ks[14], (N_EVEN, 4, POOL_GROUP, POOL_GROUP), POOL_GROUP ** -0.5),
        'pool_scale': gain(ks[15], (N_EVEN, POOL_WIDTH)),
        'w_out_even': nrm(ks[16], (N_EVEN, EVEN_MIX, D_MODEL), EVEN_MIX ** -0.5),
        'w_in_odd': nrm(ks[17], (N_ODD, D_MODEL, ODD_IN), D_MODEL ** -0.5),
        'ln_v_g': gain(ks[18], (N_ODD, CMLP_WIDTH)),
        'ln_v_b': nrm(ks[19], (N_ODD, CMLP_WIDTH), 0.02),
        'w_sp': nrm(ks[20], (N_ODD, CMLP_GROUPS, CMLP_CHUNK, CMLP_CHUNK), CMLP_CHUNK ** -0.5),
        'b_sp': gain(ks[21], (N_ODD, CMLP_GROUPS, CMLP_CHUNK)),
        'w_dw': nrm(ks[22], (N_ODD, CONV_WIDTH, CONV_CH), CONV_WIDTH ** -0.5),
        'b_dw': nrm(ks[23], (N_ODD, CONV_CH), 0.02),
        'ln_c_g': gain(ks[24], (N_ODD, CONV_CH)),
        'ln_c_b': nrm(ks[25], (N_ODD, CONV_CH), 0.02),
        'w_out_odd': nrm(ks[26], (N_ODD, ODD_MIX, D_MODEL), ODD_MIX ** -0.5),
    }


def reference(x_prompt, x_sample, cache_mla_ckv, cache_mla_krope, state_pool, state_conv,
              ln_pre, ln_post, w_in_even, g_q_lat, g_kv_lat, w_uq, w_uk, w_uv, w_pool, pool_scale,
              w_out_even, w_in_odd, ln_v_g, ln_v_b, w_sp, b_sp, w_dw, b_dw, ln_c_g, ln_c_b, w_out_odd):
    bp = x_prompt.shape[0]
    past = cache_mla_ckv.shape[2]
    pos_p = jnp.arange(x_prompt.shape[1])
    pos_s = past + jnp.arange(x_sample.shape[1])
    yp, ys = x_prompt, x_sample
    ckv_p, kr_p, pool_p, conv_p = [], [], [], []
    ckv_s, kr_s, pool_s, conv_s, v_s = [], [], [], [], []
    for layer in range(DEPTH):
        i = layer // 2
        hp = _rms(yp, ln_pre[layer])
        hs = _rms(ys, ln_pre[layer])
        if layer % 2 == 0:
            ew = (w_in_even[i], g_q_lat[i], g_kv_lat[i], w_uq[i], w_uk[i], w_uv[i],
                  w_pool[i], pool_scale[i], w_out_even[i])
            op, c1, k1, p1 = _even_layer(hp, pos_p, None, None,
                                         jnp.zeros((bp, POOL_HIST, POOL_WIDTH), hp.dtype), *ew)
            os_, c2, k2, p2 = _even_layer(hs, pos_s, cache_mla_ckv[i], cache_mla_krope[i], state_pool[i], *ew)
            ckv_p.append(c1); kr_p.append(k1); pool_p.append(p1)
            ckv_s.append(c2); kr_s.append(k2); pool_s.append(p2)
        else:
            ow = (w_in_odd[i], ln_v_g[i], ln_v_b[i], w_sp[i], b_sp[i], w_dw[i], b_dw[i],
                  ln_c_g[i], ln_c_b[i], w_out_odd[i])
            op, _, cv1 = _odd_layer(hp, jnp.zeros((bp, CONV_HIST, CONV_CH), hp.dtype), *ow)
            os_, v2, cv2 = _odd_layer(hs, state_conv[i], *ow)
            conv_p.append(cv1)
            conv_s.append(cv2); v_s.append(v2)
        yp = yp + _rms(op, ln_post[layer])
        ys = ys + _rms(os_, ln_post[layer])
    new_ckv_prompt = jnp.stack(ckv_p, axis=0)
    new_krope_prompt = jnp.stack(kr_p, axis=0)
    new_pool_prompt = jnp.stack(pool_p, axis=0)
    new_conv_prompt = jnp.stack(conv_p, axis=0)
    new_ckv_sample = jnp.stack(ckv_s, axis=0)
    new_krope_sample = jnp.stack(kr_s, axis=0)
    new_pool_sample = jnp.stack(pool_s, axis=0)
    new_conv_sample = jnp.stack(conv_s, axis=0)
    new_cmlp_v_sample = jnp.stack(v_s, axis=0)
    return (yp, ys, new_ckv_prompt, new_krope_prompt, new_pool_prompt, new_conv_prompt,
            new_ckv_sample, new_krope_sample, new_pool_sample, new_conv_sample, new_cmlp_v_sample)
```

```python
import functools

import jax
import jax.numpy as jnp
from jax import lax
from jax.experimental import pallas as pl
from jax.experimental.pallas import tpu as pltpu

F32 = jnp.float32
BF16 = jnp.bfloat16

EPS = 1e-6
NEG_INF = -1e30
CHUNK = 64
ROPE_THETA = 10000.0
POOL_WINDOWS = (2, 4, 8, 16)
POOL_HALO = 16
CONV_HALO = 32

V7X_LANES = 128
V7X_VMEM_LIMIT_BYTES = 56 * 1024 * 1024


def _cparams(ndims):
    return pltpu.CompilerParams(dimension_semantics=("arbitrary",) * ndims,
                                vmem_limit_bytes=V7X_VMEM_LIMIT_BYTES)


def _rms(x, g):
    return x * lax.rsqrt(jnp.mean(x * x, axis=-1, keepdims=True) + EPS) * g


def _ln(x, g, b):
    mu = jnp.mean(x, axis=-1, keepdims=True)
    xc = x - mu
    var = jnp.mean(xc * xc, axis=-1, keepdims=True)
    return xc * lax.rsqrt(var + EPS) * g + b


def _silu(x):
    return x * jax.nn.sigmoid(x)


def _dot(a, b):
    return jnp.dot(a, b, preferred_element_type=F32)


def _rms_cast_kernel(x_ref, g_ref, o_ref):
    o_ref[...] = _rms(x_ref[...], g_ref[...]).astype(o_ref.dtype)


def _rms_cast(x, g, tm):
    m, d = x.shape
    return pl.pallas_call(
        _rms_cast_kernel,
        grid=(m // tm,),
        in_specs=[pl.BlockSpec((tm, d), lambda i: (i, 0)),
                  pl.BlockSpec((1, d), lambda i: (0, 0))],
        out_specs=pl.BlockSpec((tm, d), lambda i: (i, 0)),
        out_shape=jax.ShapeDtypeStruct((m, d), BF16),
        compiler_params=_cparams(1),
        name="rms_cast",
    )(x, g.reshape(1, d))


def _proj(a, w_specs, extras, out_defs, epilogue, *, tm, nj, name):
    m, k = a.shape
    nw, ne, no = len(w_specs), len(extras), len(out_defs)

    def body(*refs):
        a_ref = refs[0]
        w_refs = refs[1:1 + nw]
        e_refs = refs[1 + nw:1 + nw + ne]
        o_refs = refs[1 + nw + ne:1 + nw + ne + no]
        s_refs = refs[1 + nw + ne + no:]

        @pl.when(pl.program_id(1) == 0)
        def _():
            for w_ref, s_ref in zip(w_refs, s_refs):
                s_ref[...] = w_ref[...].astype(BF16)

        av = a_ref[...].astype(BF16)
        accs = [_dot(av, s_ref[...]) for s_ref in s_refs]
        epilogue(accs, e_refs, o_refs)

    in_specs = [pl.BlockSpec((tm, k), lambda j, i: (i, 0))]
    for w, tn, f in w_specs:
        in_specs.append(pl.BlockSpec((k, tn), functools.partial(lambda j, i, f: (0, f(j)), f=f)))
    for arr, bs, im in extras:
        in_specs.append(pl.BlockSpec(bs, im))
    out_specs = [pl.BlockSpec(bs, im) for (_, _, bs, im) in out_defs]
    out_shape = [jax.ShapeDtypeStruct(sh, dt) for (sh, dt, _, _) in out_defs]
    scratch = [pltpu.VMEM((k, tn), BF16) for (_, tn, _) in w_specs]
    outs = pl.pallas_call(
        body,
        grid=(nj, m // tm),
        in_specs=in_specs,
        out_specs=out_specs,
        out_shape=out_shape,
        scratch_shapes=scratch,
        compiler_params=_cparams(2),
        name=name,
    )(a, *[w for (w, _, _) in w_specs], *[e for (e, _, _) in extras])
    return outs


def _epi_rms(out_dtype):
    def epi(accs, e_refs, o_refs):
        o_refs[0][...] = _rms(accs[0], e_refs[0][...]).astype(out_dtype)
    return epi


def _epi_silu(accs, e_refs, o_refs):
    o_refs[0][...] = _silu(accs[0]).astype(o_refs[0].dtype)


def _epi_copy(accs, e_refs, o_refs):
    o_refs[0][...] = accs[0].astype(o_refs[0].dtype)


def _epi_glu(accs, e_refs, o_refs):
    o_refs[0][...] = (accs[0] * jax.nn.sigmoid(accs[1])).astype(o_refs[0].dtype)


def _rope_pair(t):
    return t[:, :64] + t[:, 64:]


def _epi_kv(accs, e_refs, o_refs):
    o_refs[0][...] = _rms(accs[0], e_refs[0][...])
    o_refs[1][...] = _rope_pair(accs[1] * e_refs[1][...])


def _simple_proj(a, w, col0, ncols, tn, epilogue, out_dtype, *, tm, name, extras=()):
    m = a.shape[0]
    assert col0 % tn == 0 and ncols % tn == 0
    jb = col0 // tn
    return _proj(a, [(w, tn, lambda j: j + jb)], list(extras),
                 [((m, ncols), out_dtype, (tm, tn), lambda j, i: (i, j))],
                 epilogue, tm=tm, nj=ncols // tn, name=name)[0]


def _qup_kernel(cq_ref, w_ref, tab_ref, q_ref, wbf_ref, *, heads, sm_scale):
    @pl.when(pl.program_id(0) == 0)
    def _():
        wbf_ref[...] = w_ref[...].astype(BF16)

    acc = _dot(cq_ref[...], wbf_ref[...])
    tab = tab_ref[...] * sm_scale
    for h in range(heads):
        base = h * 256
        q_ref[h, :, 0:128] = (acc[:, base:base + 128] * sm_scale).astype(q_ref.dtype)
        q_ref[h, :, 128:192] = _rope_pair(acc[:, base + 128:base + 256] * tab).astype(q_ref.dtype)


def _qup(cq, wq_ext, tab, *, tm, sm_scale):
    m, kq = cq.shape
    heads = wq_ext.shape[1] // 256
    nt = tab.shape[0] // tm
    return pl.pallas_call(
        functools.partial(_qup_kernel, heads=heads, sm_scale=sm_scale),
        grid=(m // tm,),
        in_specs=[pl.BlockSpec((tm, kq), lambda i: (i, 0)),
                  pl.BlockSpec(wq_ext.shape, lambda i: (0, 0)),
                  pl.BlockSpec((tm, 128), lambda i: (i % nt, 0))],
        out_specs=pl.BlockSpec((heads, tm, 192), lambda i: (0, i, 0)),
        out_shape=jax.ShapeDtypeStruct((heads, m, 192), BF16),
        scratch_shapes=[pltpu.VMEM(wq_ext.shape, BF16)],
        compiler_params=_cparams(1),
        name="q_up",
    )(cq, wq_ext, tab)


def _kvup_kernel(ckv_ref, kr_ref, wk_ref, wv_ref, k_ref, v_ref, wkbf_ref, wvbf_ref, *, heads):
    @pl.when(pl.program_id(0) == 0)
    def _():
        wkbf_ref[...] = wk_ref[...].astype(BF16)
        wvbf_ref[...] = wv_ref[...].astype(BF16)

    c = ckv_ref[...].astype(BF16)
    kn = _dot(c, wkbf_ref[...]).astype(k_ref.dtype)
    v_ref[...] = _dot(c, wvbf_ref[...]).astype(v_ref.dtype)
    kr = kr_ref[...].astype(k_ref.dtype)
    for h in range(heads):
        k_ref[h, :, 0:128] = kn[:, h * 128:(h + 1) * 128]
        k_ref[h, :, 128:192] = kr


def _kvup(ckv, kr, w_uk, w_uv, *, tm):
    m, kc = ckv.shape
    heads = w_uk.shape[1] // 128
    return pl.pallas_call(
        functools.partial(_kvup_kernel, heads=heads),
        grid=(m // tm,),
        in_specs=[pl.BlockSpec((tm, kc), lambda i: (i, 0)),
                  pl.BlockSpec((tm, 64), lambda i: (i, 0)),
                  pl.BlockSpec(w_uk.shape, lambda i: (0, 0)),
                  pl.BlockSpec(w_uv.shape, lambda i: (0, 0))],
        out_specs=[pl.BlockSpec((heads, tm, 192), lambda i: (0, i, 0)),
                   pl.BlockSpec((tm, heads * 128), lambda i: (i, 0))],
        out_shape=[jax.ShapeDtypeStruct((heads, m, 192), BF16),
                   jax.ShapeDtypeStruct((m, heads * 128), BF16)],
        scratch_shapes=[pltpu.VMEM(w_uk.shape, BF16), pltpu.VMEM(w_uv.shape, BF16)],
        compiler_params=_cparams(1),
        name="kv_up",
    )(ckv, kr, w_uk, w_uv)


def _attn_kernel(q_ref, k_ref, v_ref, g_ref, o_ref, *, hb, tq):
    qi = pl.program_id(2)
    row = lax.broadcasted_iota(jnp.int32, (tq, tq), 0) // CHUNK
    col = lax.broadcasted_iota(jnp.int32, (tq, tq), 1) // CHUNK
    diag_mask = col <= row

    for hh in range(hb):
        q = q_ref[hh]

        def block(kb, carry, masked, hh=hh, q=q):
            m, l, acc = carry
            start = pl.multiple_of(kb * tq, tq)
            k = k_ref[hh, pl.ds(start, tq), :]
            s = lax.dot_general(q, k, (((1,), (1,)), ((), ())), preferred_element_type=F32)
            if masked:
                s = jnp.where(diag_mask, s, NEG_INF)
            m_new = jnp.maximum(m, jnp.max(s, axis=-1, keepdims=True))
            alpha = jnp.exp(m - m_new)
            p = jnp.exp(s - m_new)
            l = alpha * l + jnp.sum(p, axis=-1, keepdims=True)
            v = v_ref[pl.ds(start, tq), hh * 128:(hh + 1) * 128]
            acc = alpha * acc + _dot(p.astype(BF16), v)
            return m_new, l, acc

        init = (jnp.full((tq, 1), NEG_INF, F32), jnp.zeros((tq, 1), F32), jnp.zeros((tq, 128), F32))
        carry = lax.fori_loop(0, qi, functools.partial(block, masked=False), init)
        m, l, acc = block(qi, carry, True)
        gate = g_ref[:, hh * 128:(hh + 1) * 128].astype(F32)
        o_ref[:, hh * 128:(hh + 1) * 128] = (acc / l * gate).astype(o_ref.dtype)


def _attn_prompt(q, k, v, sga, *, batch, seq, hb, tq):
    heads, m, _ = q.shape
    nq = seq // tq
    return pl.pallas_call(
        functools.partial(_attn_kernel, hb=hb, tq=tq),
        grid=(batch, heads // hb, nq),
        in_specs=[pl.BlockSpec((hb, tq, 192), lambda b, g, i: (g, b * nq + i, 0)),
                  pl.BlockSpec((hb, seq, 192), lambda b, g, i: (g, b, 0)),
                  pl.BlockSpec((seq, hb * 128), lambda b, g, i: (b, g)),
                  pl.BlockSpec((tq, hb * 128), lambda b, g, i: (b * nq + i, g))],
        out_specs=pl.BlockSpec((tq, hb * 128), lambda b, g, i: (b * nq + i, g)),
        out_shape=jax.ShapeDtypeStruct((m, heads * 128), BF16),
        compiler_params=_cparams(3),
        name="attn_prompt",
    )(q, k, v, sga)


def _qlat_kernel(q_ref, wk_ref, o_ref):
    qn = q_ref[0, :, 0:128]
    o_ref[0] = lax.dot_general(qn, wk_ref[...].astype(BF16), (((1,), (1,)), ((), ())),
                               preferred_element_type=F32).astype(o_ref.dtype)


def _qlat(q, w_uk):
    heads, m, _ = q.shape
    kc = w_uk.shape[0]
    return pl.pallas_call(
        _qlat_kernel,
        grid=(heads,),
        in_specs=[pl.BlockSpec((1, m, 192), lambda h: (h, 0, 0)),
                  pl.BlockSpec((kc, 128), lambda h: (0, h))],
        out_specs=pl.BlockSpec((1, m, kc), lambda h: (h, 0, 0)),
        out_shape=jax.ShapeDtypeStruct((heads, m, kc), BF16),
        compiler_params=_cparams(1),
        name="q_lat",
    )(q, w_uk)


def _decode_kernel(ql_ref, q_ref, ckv_ref, kr_ref, nckv_ref, nkr_ref, o_ref, m_ref, l_ref, acc_ref, *, heads, ts):
    kb = pl.program_id(1)
    rows = heads * ts

    @pl.when(kb == 0)
    def _():
        m_ref[...] = jnp.full(m_ref.shape, NEG_INF, F32)
        l_ref[...] = jnp.zeros(l_ref.shape, F32)
        acc_ref[...] = jnp.zeros(acc_ref.shape, F32)

    ql = ql_ref[...].reshape(rows, ql_ref.shape[-1])
    qr = q_ref[:, :, 128:192].reshape(rows, 64)

    def step(ckv, kr):
        s = (lax.dot_general(ql, ckv, (((1,), (1,)), ((), ())), preferred_element_type=F32)
             + lax.dot_general(qr, kr, (((1,), (1,)), ((), ())), preferred_element_type=F32))
        m_old = m_ref[...]
        m_new = jnp.maximum(m_old, jnp.max(s, axis=-1, keepdims=True))
        alpha = jnp.exp(m_old - m_new)
        p = jnp.exp(s - m_new)
        l_ref[...] = alpha * l_ref[...] + jnp.sum(p, axis=-1, keepdims=True)
        acc_ref[...] = alpha * acc_ref[...] + _dot(p.astype(BF16), ckv)
        m_ref[...] = m_new

    step(ckv_ref[0].astype(BF16), kr_ref[0].astype(BF16))

    @pl.when(kb == pl.num_programs(1) - 1)
    def _():
        step(nckv_ref[...].astype(BF16), nkr_ref[...].astype(BF16))
        o = acc_ref[...] / l_ref[...]
        o_ref[...] = o.reshape(o_ref.shape).astype(o_ref.dtype)


def _decode_attn(qlat, q, cache_ckv, cache_kr, new_ckv, new_kr, *, ts, tk):
    heads, ms, kc = qlat.shape
    batch, past, _ = cache_ckv.shape
    return pl.pallas_call(
        functools.partial(_decode_kernel, heads=heads, ts=ts),
        grid=(batch, past // tk),
        in_specs=[pl.BlockSpec((heads, ts, kc), lambda b, j: (0, b, 0)),
                  pl.BlockSpec((heads, ts, 192), lambda b, j: (0, b, 0)),
                  pl.BlockSpec((1, tk, kc), lambda b, j: (b, j, 0)),
                  pl.BlockSpec((1, tk, 64), lambda b, j: (b, j, 0)),
                  pl.BlockSpec((ts, kc), lambda b, j: (b, 0)),
                  pl.BlockSpec((ts, 64), lambda b, j: (b, 0))],
        out_specs=pl.BlockSpec((heads, ts, kc), lambda b, j: (0, b, 0)),
        out_shape=jax.ShapeDtypeStruct((heads, ms, kc), BF16),
        scratch_shapes=[pltpu.VMEM((heads * ts, 1), F32), pltpu.VMEM((heads * ts, 1), F32),
                        pltpu.VMEM((heads * ts, kc), F32)],
        compiler_params=_cparams(2),
        name="decode_attn",
    )(qlat, q, cache_ckv, cache_kr, new_ckv, new_kr)


def _olat_kernel(ol_ref, wv_ref, g_ref, o_ref):
    o = _dot(ol_ref[0], wv_ref[...].astype(BF16))
    o_ref[...] = (o * g_ref[...].astype(F32)).astype(o_ref.dtype)


def _olat(olat, w_uv, sga):
    heads, ms, kc = olat.shape
    return pl.pallas_call(
        _olat_kernel,
        grid=(heads,),
        in_specs=[pl.BlockSpec((1, ms, kc), lambda h: (h, 0, 0)),
                  pl.BlockSpec((kc, 128), lambda h: (0, h)),
                  pl.BlockSpec((ms, 128), lambda h: (0, h))],
        out_specs=pl.BlockSpec((ms, 128), lambda h: (0, h)),
        out_shape=jax.ShapeDtypeStruct((ms, heads * 128), BF16),
        compiler_params=_cparams(1),
        name="o_lat",
    )(olat, w_uv, sga)


def _pool_kernel(u_ref, halo_ref, hist_ref, g_ref, wp_ref, ps_ref, o_ref, full_ref, *, tt, pos0, group):
    i = pl.program_id(1)
    full_ref[0:POOL_HALO, :] = jnp.where(i == 0, hist_ref[0], halo_ref[0])
    full_ref[POOL_HALO:POOL_HALO + tt, :] = u_ref[0]
    pos = pos0 + i * tt + lax.broadcasted_iota(jnp.int32, (tt, 1), 0)
    for g, w in enumerate(POOL_WINDOWS):
        cols = slice(g * group, (g + 1) * group)
        win = full_ref[pl.ds(POOL_HALO, tt), cols]
        for back in range(1, w):
            win = win + full_ref[pl.ds(POOL_HALO - back, tt), cols]
        n = jnp.minimum(pos + 1, w).astype(F32)
        d = win / n - full_ref[pl.ds(POOL_HALO, tt), cols]
        y = _dot(d.astype(BF16), wp_ref[g].astype(BF16))
        y = y * ps_ref[:, cols] * g_ref[0, :, cols].astype(F32)
        o_ref[0, :, cols] = y.astype(o_ref.dtype)


def _pool_mix(u, hist16, sgb, w_pool, pool_scale, *, tt, pos0):
    batch, t, width = u.shape
    group = width // len(POOL_WINDOWS)
    hpb = tt // POOL_HALO
    return pl.pallas_call(
        functools.partial(_pool_kernel, tt=tt, pos0=pos0, group=group),
        grid=(batch, t // tt),
        in_specs=[pl.BlockSpec((1, tt, width), lambda b, i: (b, i, 0)),
                  pl.BlockSpec((1, POOL_HALO, width), lambda b, i: (b, jnp.maximum(i * hpb - 1, 0), 0)),
                  pl.BlockSpec((1, POOL_HALO, width), lambda b, i: (b, 0, 0)),
                  pl.BlockSpec((1, tt, width), lambda b, i: (b, i, 0)),
                  pl.BlockSpec(w_pool.shape, lambda b, i: (0, 0, 0)),
                  pl.BlockSpec((1, width), lambda b, i: (0, 0))],
        out_specs=pl.BlockSpec((1, tt, width), lambda b, i: (b, i, 0)),
        out_shape=jax.ShapeDtypeStruct((batch, t, width), BF16),
        scratch_shapes=[pltpu.VMEM((POOL_HALO + tt, width), F32)],
        compiler_params=_cparams(2),
        name="pool_mix",
    )(u, u, hist16, sgb, w_pool, pool_scale.reshape(1, width))


def _gmlp_kernel(uv_ref, v_ref, g_ref, wsp_ref, bsp_ref, lg_ref, lb_ref, *out_refs, tt, chunk, groups, emit_vn):
    o_ref = out_refs[0]
    width = o_ref.shape[-1]
    gw = width // groups
    vn = _ln(v_ref[...].astype(F32), lg_ref[...], lb_ref[...])
    if emit_vn:
        out_refs[1][...] = vn
    vnb = vn.astype(BF16)
    tri = (lax.broadcasted_iota(jnp.int32, (chunk, chunk), 1)
           <= lax.broadcasted_iota(jnp.int32, (chunk, chunk), 0))
    for g in range(groups):
        w = jnp.where(tri, wsp_ref[g], 0.0).astype(BF16)
        bias = bsp_ref[g]
        cols = slice(g * gw, (g + 1) * gw)
        for c in range(tt // chunk):
            rows = slice(c * chunk, (c + 1) * chunk)
            mixed = _dot(w, vnb[rows, cols]) + bias
            o_ref[rows, cols] = (uv_ref[rows, cols].astype(F32) * mixed
                                 * g_ref[rows, cols].astype(F32)).astype(o_ref.dtype)


def _gmlp(uv, sgc, w_sp, b_sp, ln_g, ln_b, *, tt, chunk, emit_vn):
    m, width = sgc.shape
    groups = w_sp.shape[0]
    out_specs = [pl.BlockSpec((tt, width), lambda i: (i, 0))]
    out_shape = [jax.ShapeDtypeStruct((m, width), BF16)]
    if emit_vn:
        out_specs.append(pl.BlockSpec((tt, width), lambda i: (i, 0)))
        out_shape.append(jax.ShapeDtypeStruct((m, width), F32))
    return pl.pallas_call(
        functools.partial(_gmlp_kernel, tt=tt, chunk=chunk, groups=groups, emit_vn=emit_vn),
        grid=(m // tt,),
        in_specs=[pl.BlockSpec((tt, width), lambda i: (i, 0)),
                  pl.BlockSpec((tt, width), lambda i: (i, 1)),
                  pl.BlockSpec((tt, width), lambda i: (i, 0)),
                  pl.BlockSpec((groups, chunk, chunk), lambda i: (0, 0, 0)),
                  pl.BlockSpec((groups, chunk, 1), lambda i: (0, 0, 0)),
                  pl.BlockSpec((1, width), lambda i: (0, 0)),
                  pl.BlockSpec((1, width), lambda i: (0, 0))],
        out_specs=out_specs,
        out_shape=out_shape,
        compiler_params=_cparams(1),
        name="gmlp",
    )(uv, uv, sgc, w_sp, b_sp, ln_g.reshape(1, width), ln_b.reshape(1, width))


def _conv_kernel(x_ref, halo_ref, hist_ref, g_ref, w_ref, b_ref, lg_ref, lb_ref, o_ref, full_ref, y_ref,
                 *, tt, taps, rb):
    i = pl.program_id(1)
    ch = x_ref.shape[-1]
    if halo_ref.shape[1] == CONV_HALO:
        full_ref[0:CONV_HALO, :] = jnp.where(i == 0, hist_ref[0], halo_ref[0])
    else:
        full_ref[0:CONV_HALO, :] = hist_ref[0]
    full_ref[CONV_HALO:CONV_HALO + tt, :] = x_ref[0]
    first = CONV_HALO - (taps - 1)
    for r0 in range(0, tt, rb):
        for c0 in range(0, ch, V7X_LANES):
            cols = slice(c0, c0 + V7X_LANES)
            acc = full_ref[pl.ds(first + r0, rb), cols] * w_ref[0:1, cols]
            for k in range(1, taps):
                acc = acc + full_ref[pl.ds(first + r0 + k, rb), cols] * w_ref[k:k + 1, cols]
            y_ref[r0:r0 + rb, cols] = acc
    y = _ln(y_ref[...] + b_ref[...], lg_ref[...], lb_ref[...])
    o_ref[0] = (_silu(y) * g_ref[0].astype(F32)).astype(o_ref.dtype)


def _conv_module(x, hist32, sgd, w_dw, b_dw, ln_g, ln_b, *, tt):
    batch, t, ch = x.shape
    taps = w_dw.shape[0]
    hpb = max(tt // CONV_HALO, 1)
    rb = min(tt, 32)
    return pl.pallas_call(
        functools.partial(_conv_kernel, tt=tt, taps=taps, rb=rb),
        grid=(batch, t // tt),
        in_specs=[pl.BlockSpec((1, tt, ch), lambda b, i: (b, i, 0)),
                  pl.BlockSpec((1, min(CONV_HALO, t), ch), lambda b, i: (b, jnp.maximum(i * hpb - 1, 0), 0)),
                  pl.BlockSpec((1, CONV_HALO, ch), lambda b, i: (b, 0, 0)),
                  pl.BlockSpec((1, tt, ch), lambda b, i: (b, i, 0)),
                  pl.BlockSpec((taps, ch), lambda b, i: (0, 0)),
                  pl.BlockSpec((1, ch), lambda b, i: (0, 0)),
                  pl.BlockSpec((1, ch), lambda b, i: (0, 0)),
                  pl.BlockSpec((1, ch), lambda b, i: (0, 0))],
        out_specs=pl.BlockSpec((1, tt, ch), lambda b, i: (b, i, 0)),
        out_shape=jax.ShapeDtypeStruct((batch, t, ch), BF16),
        scratch_shapes=[pltpu.VMEM((CONV_HALO + tt, ch), F32), pltpu.VMEM((tt, ch), F32)],
        compiler_params=_cparams(2),
        name="conv_module",
    )(x, x, hist32, sgd, w_dw, b_dw.reshape(1, ch), ln_g.reshape(1, ch), ln_b.reshape(1, ch))


def _outproj_kernel(a_ref, b_ref, y_ref, w_ref, gpost_ref, gnext_ref, ynew_ref, *h_refs):
    ka = a_ref.shape[1]
    acc = _dot(a_ref[...], w_ref[0:ka, :]) + _dot(b_ref[...], w_ref[ka:, :])
    yn = y_ref[...] + _rms(acc, gpost_ref[...])
    ynew_ref[...] = yn
    if h_refs:
        h_refs[0][...] = _rms(yn, gnext_ref[...]).astype(h_refs[0].dtype)


def _outproj(a, b, y, w_bf, g_post, g_next, *, tm):
    m, d = y.shape
    ka, kb = a.shape[1], b.shape[1]
    emit_h = g_next is not None
    gn = g_next if emit_h else g_post
    out_specs = [pl.BlockSpec((tm, d), lambda i: (i, 0))]
    out_shape = [jax.ShapeDtypeStruct((m, d), F32)]
    if emit_h:
        out_specs.append(pl.BlockSpec((tm, d), lambda i: (i, 0)))
        out_shape.append(jax.ShapeDtypeStruct((m, d), BF16))
    outs = pl.pallas_call(
        _outproj_kernel,
        grid=(m // tm,),
        in_specs=[pl.BlockSpec((tm, ka), lambda i: (i, 0)),
                  pl.BlockSpec((tm, kb), lambda i: (i, 0)),
                  pl.BlockSpec((tm, d), lambda i: (i, 0)),
                  pl.BlockSpec((ka + kb, d), lambda i: (0, 0)),
                  pl.BlockSpec((1, d), lambda i: (0, 0)),
                  pl.BlockSpec((1, d), lambda i: (0, 0))],
        out_specs=out_specs,
        out_shape=out_shape,
        compiler_params=_cparams(1),
        name="out_proj",
    )(a, b, y, w_bf, g_post.reshape(1, d), gn.reshape(1, d))
    return (outs[0], outs[1]) if emit_h else (outs[0], None)


def _rope_table(pos, half):
    inv = jnp.power(jnp.float32(ROPE_THETA), -jnp.arange(half, dtype=jnp.float32) / half)
    ang = pos.astype(jnp.float32)[:, None] * inv[None, :]
    cos, sin = jnp.cos(ang), jnp.sin(ang)
    return jnp.concatenate([cos, cos, -sin, sin], axis=-1)


def _swap_halves(w):
    half = w.shape[-1] // 2
    return jnp.concatenate([w[..., half:], w[..., :half]], axis=-1)


def _even_weights(w_in, w_uq, w_uk, w_uv, dims):
    q_lora, kv_lora, qk_rope, mla_width = dims
    kr0 = q_lora + kv_lora
    w_kr = w_in[:, kr0:kr0 + qk_rope]
    w_kr_ext = jnp.concatenate([w_kr, _swap_halves(w_kr)], axis=1)
    w_rest = w_in[:, kr0 + qk_rope:]
    nope = w_uq.shape[-1] - qk_rope
    wq_ext = jnp.concatenate([w_uq, _swap_halves(w_uq[..., nope:])], axis=-1)
    wq_ext = wq_ext.reshape(q_lora, -1)
    return w_kr_ext, w_rest, wq_ext, w_uk.reshape(kv_lora, -1), w_uv.reshape(kv_lora, -1)


def _even_layer(h, y, tab, batch, t, past_kv, pool_hist16, wts, g_post, g_next, *, pos0, tm, sm_scale):
    (w_in, g_q, g_kv, w_kr_ext, w_rest, wq_ext, w_uk2, w_uv2, w_pool, pool_scale, w_out_bf) = wts
    m = h.shape[0]
    q_lora, kv_lora = g_q.shape[0], g_kv.shape[0]
    mla_width = w_uv2.shape[1]
    pool_width = pool_scale.shape[0]
    nt = tab.shape[0] // tm

    cq = _simple_proj(h, w_in, 0, q_lora, q_lora, _epi_rms(BF16), BF16, tm=tm, name="in_cq",
                      extras=[(g_q.reshape(1, q_lora), (1, q_lora), lambda j, i: (0, 0))])
    ckv, kr = _proj(h, [(w_in, kv_lora, lambda j: q_lora // kv_lora), (w_kr_ext, 128, lambda j: 0)],
                    [(g_kv.reshape(1, kv_lora), (1, kv_lora), lambda j, i: (0, 0)),
                     (tab, (tm, 128), lambda j, i: (i % nt, 0))],
                    [((m, kv_lora), F32, (tm, kv_lora), lambda j, i: (i, 0)),
                     ((m, 64), F32, (tm, 64), lambda j, i: (i, 0))],
                    _epi_kv, tm=tm, nj=1, name="in_ckv")
    tn = 1024
    sga = _simple_proj(h, w_rest, 0, mla_width, tn, _epi_silu, BF16, tm=tm, name="in_ga")
    u_b = _simple_proj(h, w_rest, mla_width, pool_width, tn, _epi_copy, F32, tm=tm, name="in_ub")
    sgb = _simple_proj(h, w_rest, mla_width + pool_width, pool_width, tn, _epi_silu, BF16, tm=tm, name="in_gb")

    q = _qup(cq, wq_ext, tab, tm=min(tm, 512), sm_scale=sm_scale)
    if past_kv is None:
        k, v = _kvup(ckv, kr, w_uk2, w_uv2, tm=min(tm, 512))
        a = _attn_prompt(q, k, v, sga, batch=batch, seq=t, hb=4, tq=256)
    else:
        past_ckv, past_kr = past_kv
        assert past_ckv.shape[1] % CHUNK == 0 and t <= CHUNK
        qlat = _qlat(q, w_uk2)
        olat = _decode_attn(qlat, q, past_ckv, past_kr, ckv, kr, ts=t, tk=min(1024, past_ckv.shape[1]))
        a = _olat(olat, w_uv2, sga)

    u3 = u_b.reshape(batch, t, pool_width)
    b = _pool_mix(u3, pool_hist16, sgb.reshape(batch, t, pool_width), w_pool, pool_scale,
                  tt=min(t, 256), pos0=pos0).reshape(m, pool_width)
    y_new, h_next = _outproj(a, b, y, w_out_bf, g_post, g_next, tm=min(tm, 256))
    return y_new, h_next, ckv, kr, u3


def _odd_layer(h, y, batch, t, conv_hist32, wts, g_post, g_next, *, tm, emit_vn):
    (w_in, ln_v_g, ln_v_b, w_sp, b_sp, w_dw, b_dw, ln_c_g, ln_c_b, w_out_bf) = wts
    m = h.shape[0]
    width = ln_v_g.shape[0]
    ch = ln_c_g.shape[0]
    tn = 1024
    uv = _simple_proj(h, w_in, 0, 2 * width, tn, _epi_copy, BF16, tm=tm, name="in_uv")
    sgc = _simple_proj(h, w_in, 2 * width, width, tn, _epi_silu, BF16, tm=tm, name="in_gc")
    tg = 512
    ja, jb = (3 * width) // tg, (3 * width + ch) // tg
    x_c = _proj(h, [(w_in, tg, lambda j: j + ja), (w_in, tg, lambda j: j + jb)], [],
                [((m, ch), F32, (tm, tg), lambda j, i: (i, j))],
                _epi_glu, tm=tm, nj=ch // tg, name="in_glu")[0]
    sgd = _simple_proj(h, w_in, 3 * width + 2 * ch, ch, tn, _epi_silu, BF16, tm=tm, name="in_gd")

    chunk = w_sp.shape[-1]
    gm = _gmlp(uv, sgc, w_sp, b_sp, ln_v_g, ln_v_b, tt=min(m, 512), chunk=chunk, emit_vn=emit_vn)
    c, vn = (gm[0], gm[1]) if emit_vn else (gm[0], None)
    x3 = x_c.reshape(batch, t, ch)
    yc = _conv_module(x3, conv_hist32, sgd.reshape(batch, t, ch), w_dw, b_dw, ln_c_g, ln_c_b,
                      tt=min(t, 128)).reshape(m, ch)
    y_new, h_next = _outproj(c, yc, y, w_out_bf, g_post, g_next, tm=min(tm, 256))
    return y_new, h_next, x3, vn


def kernel(x_prompt, x_sample, cache_mla_ckv, cache_mla_krope, state_pool, state_conv, ln_pre, ln_post,
           w_in_even, g_q_lat, g_kv_lat, w_uq, w_uk, w_uv, w_pool, pool_scale, w_out_even, w_in_odd,
           ln_v_g, ln_v_b, w_sp, b_sp, w_dw, b_dw, ln_c_g, ln_c_b, w_out_odd):
    bp, sp, d = x_prompt.shape
    bs, ts, _ = x_sample.shape
    past = cache_mla_ckv.shape[2]
    depth = ln_pre.shape[0]
    q_lora, kv_lora = g_q_lat.shape[1], g_kv_lat.shape[1]
    qk_rope = cache_mla_krope.shape[-1]
    qk_nope = w_uq.shape[-1] - qk_rope
    mla_width = w_uv.shape[2] * w_uv.shape[3]
    pool_width = pool_scale.shape[1]
    pool_hist = state_pool.shape[2]
    conv_hist = state_conv.shape[2]
    ch = state_conv.shape[3]
    sm_scale = float((qk_nope + qk_rope) ** -0.5)
    mp, ms = bp * sp, bs * ts
    tmp, tms = min(1024, sp), ms

    tab_p = _rope_table(jnp.arange(sp), qk_rope // 2)
    tab_s = jnp.tile(_rope_table(past + jnp.arange(ts), qk_rope // 2), (bs, 1))

    yp = x_prompt.reshape(mp, d)
    ys = x_sample.reshape(ms, d)
    hp = _rms_cast(yp, ln_pre[0], tmp)
    hs = _rms_cast(ys, ln_pre[0], tms)

    ckv_p, kr_p, pool_p, conv_p = [], [], [], []
    ckv_s, kr_s, pool_s, conv_s, v_s = [], [], [], [], []
    for layer in range(depth):
        i = layer // 2
        g_post = ln_post[layer]
        g_next = ln_pre[layer + 1] if layer + 1 < depth else None
        if layer % 2 == 0:
            w_kr_ext, w_rest, wq_ext, w_uk2, w_uv2 = _even_weights(
                w_in_even[i], w_uq[i], w_uk[i], w_uv[i], (q_lora, kv_lora, qk_rope, mla_width))
            wts = (w_in_even[i], g_q_lat[i], g_kv_lat[i], w_kr_ext, w_rest, wq_ext, w_uk2, w_uv2,
                   w_pool[i], pool_scale[i], w_out_even[i].astype(BF16))
            zero_hist = jnp.zeros((bp, POOL_HALO, pool_width), F32)
            yp, hp, c1, k1, u1 = _even_layer(hp, yp, tab_p, bp, sp, None, zero_hist, wts, g_post, g_next,
                                             pos0=0, tm=tmp, sm_scale=sm_scale)
            hist16 = jnp.pad(state_pool[i], ((0, 0), (POOL_HALO - pool_hist, 0), (0, 0)))
            ys, hs, c2, k2, u2 = _even_layer(hs, ys, tab_s, bs, ts, (cache_mla_ckv[i], cache_mla_krope[i]),
                                             hist16, wts, g_post, g_next, pos0=past, tm=tms, sm_scale=sm_scale)
            ckv_p.append(c1.reshape(bp, sp, kv_lora))
            kr_p.append(k1.reshape(bp, sp, qk_rope))
            pool_p.append(u1[:, sp - pool_hist:])
            ckv_s.append(c2.reshape(bs, ts, kv_lora))
            kr_s.append(k2.reshape(bs, ts, qk_rope))
            pool_s.append(jnp.concatenate([state_pool[i], u2], axis=1)[:, -pool_hist:])
        else:
            wts = (w_in_odd[i], ln_v_g[i], ln_v_b[i], w_sp[i], b_sp[i][:, :, None], w_dw[i], b_dw[i],
                   ln_c_g[i], ln_c_b[i], w_out_odd[i].astype(BF16))
            zero_hist = jnp.zeros((bp, CONV_HALO, ch), F32)
            yp, hp, x1, _ = _odd_layer(hp, yp, bp, sp, zero_hist, wts, g_post, g_next, tm=tmp, emit_vn=False)
            lc = min(ts, w_sp.shape[-1])
            wts_s = wts[:3] + (w_sp[i][:, :lc, :lc], b_sp[i][:, :lc, None]) + wts[5:]
            hist32 = jnp.pad(state_conv[i], ((0, 0), (CONV_HALO - conv_hist, 0), (0, 0)))
            ys, hs, x2, v2 = _odd_layer(hs, ys, bs, ts, hist32, wts_s, g_post, g_next, tm=tms, emit_vn=True)
            conv_p.append(x1[:, sp - conv_hist:])
            conv_s.append(jnp.concatenate([state_conv[i], x2], axis=1)[:, -conv_hist:])
            v_s.append(v2.reshape(bs, ts, -1))

    return (yp.reshape(bp, sp, d), ys.reshape(bs, ts, d),
            jnp.stack(ckv_p), jnp.stack(kr_p), jnp.stack(pool_p), jnp.stack(conv_p),
            jnp.stack(ckv_s), jnp.stack(kr_s), jnp.stack(pool_s), jnp.stack(conv_s), jnp.stack(v_s))
```

```python
import functools
import math

import jax
import jax.numpy as jnp
from jax import lax
from jax.experimental import pallas as pl
from jax.experimental.pallas import tpu as pltpu

F32 = jnp.float32
BF16 = jnp.bfloat16

EPS = 1e-6
NEG_INF = -1e30
CHUNK = 64
ROPE_THETA = 10000.0
POOL_WINDOWS = (2, 4, 8, 16)
POOL_HALO = 16
CONV_HALO = 32
ATTN_SCORE_LOOKAHEAD = 4

V7X_LANES = 128
V7X_SUBLANES = 8
V7X_VMEM_LIMIT_BYTES = 56 * 1024 * 1024


def _cparams(ndims):
    return pltpu.CompilerParams(dimension_semantics=("arbitrary",) * ndims,
                                vmem_limit_bytes=V7X_VMEM_LIMIT_BYTES)


def _rms(x, g):
    return x * lax.rsqrt(jnp.mean(x * x, axis=-1, keepdims=True) + EPS) * g


def _ln(x, g, b):
    mu = jnp.mean(x, axis=-1, keepdims=True)
    xc = x - mu
    var = jnp.mean(xc * xc, axis=-1, keepdims=True)
    return xc * lax.rsqrt(var + EPS) * g + b


def _silu(x):
    return x * jax.nn.sigmoid(x)


def _dot(a, b):
    return jnp.dot(a, b, preferred_element_type=F32)


def _rms_cast_kernel(x_ref, g_ref, o_ref):
    o_ref[...] = _rms(x_ref[...], g_ref[...]).astype(o_ref.dtype)


def _rms_cast(x, g, tm):
    m, d = x.shape
    return pl.pallas_call(
        _rms_cast_kernel,
        grid=(m // tm,),
        in_specs=[pl.BlockSpec((tm, d), lambda i: (i, 0)),
                  pl.BlockSpec((1, d), lambda i: (0, 0))],
        out_specs=pl.BlockSpec((tm, d), lambda i: (i, 0)),
        out_shape=jax.ShapeDtypeStruct((m, d), BF16),
        compiler_params=_cparams(1),
        name="rms_cast",
    )(x, g.reshape(1, d))


def _w_block_spec(w, layer, k, width, block_of_j):
    if layer is None:
        return pl.BlockSpec((k, width), lambda j, i: (0, block_of_j(j)))
    return pl.BlockSpec((None, k, width), lambda j, i: (layer, 0, block_of_j(j)))


def _proj(a, w_specs, extras, out_defs, epilogue, *, tm, nj, name):
    m, k = a.shape
    nw, ne, no = len(w_specs), len(extras), len(out_defs)
    in_specs = [pl.BlockSpec((tm, k), lambda j, i: (i, 0))]
    w_args, lane_offs = [], []
    for w, layer, tn, col0 in w_specs:
        off = col0 % V7X_LANES
        base = col0 - off
        assert base % tn == 0 and tn % V7X_LANES == 0
        lane_offs.append(off)
        in_specs.append(_w_block_spec(w, layer, k, tn, functools.partial(lambda j, b: j + b, b=base // tn)))
        w_args.append(w)
        if off:
            nxt = (base + tn) // V7X_LANES
            in_specs.append(_w_block_spec(w, layer, k, V7X_LANES,
                                          functools.partial(lambda j, b, s: j * s + b, b=nxt, s=tn // V7X_LANES)))
            w_args.append(w)
    nwin = len(w_args)

    def body(*refs):
        a_ref = refs[0]
        w_refs = refs[1:1 + nwin]
        e_refs = refs[1 + nwin:1 + nwin + ne]
        o_refs = refs[1 + nwin + ne:1 + nwin + ne + no]
        s_refs = refs[1 + nwin + ne + no:]

        @pl.when(pl.program_id(1) == 0)
        def _():
            w_iter = iter(w_refs)
            for off, s_ref in zip(lane_offs, s_refs):
                tile = next(w_iter)[...].astype(BF16)
                if off:
                    tile = jnp.concatenate([tile, next(w_iter)[...].astype(BF16)], axis=1)
                    tile = tile[:, off:off + s_ref.shape[1]]
                s_ref[...] = tile

        av = a_ref[...].astype(BF16)
        accs = [_dot(av, s_ref[...]) for s_ref in s_refs]
        epilogue(accs, e_refs, o_refs)

    for arr, bs, im in extras:
        in_specs.append(pl.BlockSpec(bs, im))
    out_specs = [pl.BlockSpec(bs, im) for (_, _, bs, im) in out_defs]
    out_shape = [jax.ShapeDtypeStruct(sh, dt) for (sh, dt, _, _) in out_defs]
    scratch = [pltpu.VMEM((k, tn), BF16) for (_, _, tn, _) in w_specs]
    outs = pl.pallas_call(
        body,
        grid=(nj, m // tm),
        in_specs=in_specs,
        out_specs=out_specs,
        out_shape=out_shape,
        scratch_shapes=scratch,
        compiler_params=_cparams(2),
        name=name,
    )(a, *w_args, *[e for (e, _, _) in extras])
    return outs


def _epi_rms(out_dtype):
    def epi(accs, e_refs, o_refs):
        o_refs[0][...] = _rms(accs[0], e_refs[0][...]).astype(out_dtype)
    return epi


def _epi_silu(accs, e_refs, o_refs):
    o_refs[0][...] = _silu(accs[0]).astype(o_refs[0].dtype)


def _epi_copy(accs, e_refs, o_refs):
    o_refs[0][...] = accs[0].astype(o_refs[0].dtype)


def _epi_glu(accs, e_refs, o_refs):
    o_refs[0][...] = (accs[0] * jax.nn.sigmoid(accs[1])).astype(o_refs[0].dtype)


def _rope_pair(t):
    return t[:, :64] + t[:, 64:]


def _epi_kv(accs, e_refs, o_refs):
    o_refs[0][...] = _rms(accs[0], e_refs[0][...])
    o_refs[1][...] = _rope_pair(accs[1] * e_refs[1][...])


def _simple_proj(a, w, layer, col0, ncols, tn, epilogue, out_dtype, *, tm, name, extras=()):
    m = a.shape[0]
    assert ncols % tn == 0
    return _proj(a, [(w, layer, tn, col0)], list(extras),
                 [((m, ncols), out_dtype, (tm, tn), lambda j, i: (i, j))],
                 epilogue, tm=tm, nj=ncols // tn, name=name)[0]


def _qup_kernel(cq_ref, w_ref, tab_ref, q_ref, wbf_ref, *, heads, sm_scale):
    @pl.when(pl.program_id(0) == 0)
    def _():
        wbf_ref[...] = w_ref[...].astype(BF16)

    acc = _dot(cq_ref[...], wbf_ref[...])
    tab = tab_ref[...] * sm_scale
    for h in range(heads):
        base = h * 256
        q_ref[h, :, 0:128] = (acc[:, base:base + 128] * sm_scale).astype(q_ref.dtype)
        q_ref[h, :, 128:192] = _rope_pair(acc[:, base + 128:base + 256] * tab).astype(q_ref.dtype)


def _qup(cq, wq_ext, tab, *, tm, sm_scale):
    m, kq = cq.shape
    heads = wq_ext.shape[1] // 256
    nt = tab.shape[0] // tm
    return pl.pallas_call(
        functools.partial(_qup_kernel, heads=heads, sm_scale=sm_scale),
        grid=(m // tm,),
        in_specs=[pl.BlockSpec((tm, kq), lambda i: (i, 0)),
                  pl.BlockSpec(wq_ext.shape, lambda i: (0, 0)),
                  pl.BlockSpec((tm, 128), lambda i: (i % nt, 0))],
        out_specs=pl.BlockSpec((heads, tm, 192), lambda i: (0, i, 0)),
        out_shape=jax.ShapeDtypeStruct((heads, m, 192), BF16),
        scratch_shapes=[pltpu.VMEM(wq_ext.shape, BF16)],
        compiler_params=_cparams(1),
        name="q_up",
    )(cq, wq_ext, tab)


def _kvup_kernel(ckv_ref, kr_ref, wk_ref, wv_ref, k_ref, v_ref, wkbf_ref, wvbf_ref, *, heads):
    @pl.when(pl.program_id(0) == 0)
    def _():
        wkbf_ref[...] = wk_ref[...].astype(BF16)
        wvbf_ref[...] = wv_ref[...].astype(BF16)

    c = ckv_ref[...].astype(BF16)
    kn = _dot(c, wkbf_ref[...]).astype(k_ref.dtype)
    v_ref[...] = lax.dot_general(wvbf_ref[...], c, (((1,), (1,)), ((), ())),
                                 preferred_element_type=F32).astype(v_ref.dtype)
    kr = kr_ref[...].astype(k_ref.dtype)
    for h in range(heads):
        k_ref[h, :, 0:128] = kn[:, h * 128:(h + 1) * 128]
        k_ref[h, :, 128:192] = kr


def _kvup(ckv, kr, w_uk, w_uv_t, *, tm):
    m, kc = ckv.shape
    heads = w_uk.shape[1] // 128
    return pl.pallas_call(
        functools.partial(_kvup_kernel, heads=heads),
        grid=(m // tm,),
        in_specs=[pl.BlockSpec((tm, kc), lambda i: (i, 0)),
                  pl.BlockSpec((tm, 64), lambda i: (i, 0)),
                  pl.BlockSpec(w_uk.shape, lambda i: (0, 0)),
                  pl.BlockSpec(w_uv_t.shape, lambda i: (0, 0))],
        out_specs=[pl.BlockSpec((heads, tm, 192), lambda i: (0, i, 0)),
                   pl.BlockSpec((heads * 128, tm), lambda i: (0, i))],
        out_shape=[jax.ShapeDtypeStruct((heads, m, 192), BF16),
                   jax.ShapeDtypeStruct((heads * 128, m), BF16)],
        scratch_shapes=[pltpu.VMEM(w_uk.shape, BF16), pltpu.VMEM(w_uv_t.shape, BF16)],
        compiler_params=_cparams(1),
        name="kv_up",
    )(ckv, kr, w_uk, w_uv_t)


def _attn_kernel(q_ref, k_ref, vt_ref, g_ref, o_ref, m_ref, l_ref, acc_ref, *, hb, tq):
    qi = pl.program_id(2)
    key_chunk = lax.broadcasted_iota(jnp.int32, (tq, tq), 0) // CHUNK
    qry_chunk = lax.broadcasted_iota(jnp.int32, (tq, tq), 1) // CHUNK
    diag_mask = key_chunk <= qry_chunk

    m_ref[...] = jnp.full(m_ref.shape, NEG_INF, F32)
    l_ref[...] = jnp.zeros(l_ref.shape, F32)
    acc_ref[...] = jnp.zeros(acc_ref.shape, F32)

    def block(kb, masked):
        start = pl.multiple_of(kb * tq, tq)

        def scores(hh):
            k = k_ref[hh, pl.ds(start, tq), :]
            return lax.dot_general(k, q_ref[hh], (((1,), (1,)), ((), ())), preferred_element_type=F32)

        def softmax(hh, st):
            if masked:
                st = jnp.where(diag_mask, st, NEG_INF)
            m_old = m_ref[hh]
            m_new = jnp.maximum(m_old, jnp.max(st, axis=0, keepdims=True))
            alpha = jnp.exp2(m_old - m_new)
            p = jnp.exp2(st - m_new)
            l_ref[hh] = alpha * l_ref[hh] + jnp.sum(p, axis=0, keepdims=True)
            m_ref[hh] = m_new
            return alpha, p.astype(BF16)

        def accumulate(hh, alpha, p):
            vt = vt_ref[hh * 128:(hh + 1) * 128, pl.ds(start, tq)]
            acc_ref[hh] = alpha * acc_ref[hh] + _dot(vt, p)

        ahead = min(ATTN_SCORE_LOOKAHEAD, hb)
        st = {hh: scores(hh) for hh in range(ahead)}
        pending = None
        for hh in range(hb):
            if hh + ahead < hb:
                st[hh + ahead] = scores(hh + ahead)
            alpha, p = softmax(hh, st.pop(hh))
            if pending is not None:
                accumulate(*pending)
            pending = (hh, alpha, p)
        accumulate(*pending)

    def body(kb, carry):
        block(kb, False)
        return carry

    lax.fori_loop(0, qi, body, 0)
    block(qi, True)
    for hh in range(hb):
        o = (acc_ref[hh] / l_ref[hh]).T
        gate = g_ref[:, hh * 128:(hh + 1) * 128].astype(F32)
        o_ref[:, hh * 128:(hh + 1) * 128] = (o * gate).astype(o_ref.dtype)


def _attn_prompt(q, k, vt, sga, *, batch, seq, hb, tq):
    heads, m, _ = q.shape
    nq = seq // tq
    return pl.pallas_call(
        functools.partial(_attn_kernel, hb=hb, tq=tq),
        grid=(batch, heads // hb, nq),
        in_specs=[pl.BlockSpec((hb, tq, 192), lambda b, g, i: (g, b * nq + i, 0)),
                  pl.BlockSpec((hb, seq, 192), lambda b, g, i: (g, b, 0)),
                  pl.BlockSpec((hb * 128, seq), lambda b, g, i: (g, b)),
                  pl.BlockSpec((tq, hb * 128), lambda b, g, i: (b * nq + i, g))],
        out_specs=pl.BlockSpec((tq, hb * 128), lambda b, g, i: (b * nq + i, g)),
        out_shape=jax.ShapeDtypeStruct((m, heads * 128), BF16),
        scratch_shapes=[pltpu.VMEM((hb, 1, tq), F32), pltpu.VMEM((hb, 1, tq), F32),
                        pltpu.VMEM((hb, 128, tq), F32)],
        compiler_params=_cparams(3),
        name="attn_prompt",
    )(q, k, vt, sga)


def _qlat_kernel(q_ref, wk_ref, o_ref):
    qn = q_ref[0, :, 0:128]
    o_ref[0] = lax.dot_general(qn, wk_ref[...].astype(BF16), (((1,), (1,)), ((), ())),
                               preferred_element_type=F32).astype(o_ref.dtype)


def _qlat(q, w_uk):
    heads, m, _ = q.shape
    kc = w_uk.shape[0]
    return pl.pallas_call(
        _qlat_kernel,
        grid=(heads,),
        in_specs=[pl.BlockSpec((1, m, 192), lambda h: (h, 0, 0)),
                  pl.BlockSpec((kc, 128), lambda h: (0, h))],
        out_specs=pl.BlockSpec((1, m, kc), lambda h: (h, 0, 0)),
        out_shape=jax.ShapeDtypeStruct((heads, m, kc), BF16),
        compiler_params=_cparams(1),
        name="q_lat",
    )(q, w_uk)


def _decode_kernel(ql_ref, q_ref, ckv_ref, kr_ref, nckv_ref, nkr_ref, o_ref, m_ref, l_ref, acc_ref, *, heads, ts):
    kb = pl.program_id(1)
    rows = heads * ts

    @pl.when(kb == 0)
    def _():
        m_ref[...] = jnp.full(m_ref.shape, NEG_INF, F32)
        l_ref[...] = jnp.zeros(l_ref.shape, F32)
        acc_ref[...] = jnp.zeros(acc_ref.shape, F32)

    ql = ql_ref[...].reshape(rows, ql_ref.shape[-1])
    qr = q_ref[:, :, 128:192].reshape(rows, 64)

    def step(ckv, kr):
        s = (lax.dot_general(ql, ckv, (((1,), (1,)), ((), ())), preferred_element_type=F32)
             + lax.dot_general(qr, kr, (((1,), (1,)), ((), ())), preferred_element_type=F32))
        m_old = m_ref[...]
        m_new = jnp.maximum(m_old, jnp.max(s, axis=-1, keepdims=True))
        alpha = jnp.exp2(m_old - m_new)
        p = jnp.exp2(s - m_new)
        l_ref[...] = alpha * l_ref[...] + jnp.sum(p, axis=-1, keepdims=True)
        acc_ref[...] = alpha * acc_ref[...] + _dot(p.astype(BF16), ckv)
        m_ref[...] = m_new

    step(ckv_ref[0].astype(BF16), kr_ref[0].astype(BF16))

    @pl.when(kb == pl.num_programs(1) - 1)
    def _():
        step(nckv_ref[...].astype(BF16), nkr_ref[...].astype(BF16))
        o = acc_ref[...] / l_ref[...]
        o_ref[...] = o.reshape(o_ref.shape).astype(o_ref.dtype)


def _decode_attn(qlat, q, cache_ckv, cache_kr, layer, new_ckv, new_kr, *, ts, tk):
    heads, ms, kc = qlat.shape
    _, batch, past, _ = cache_ckv.shape
    return pl.pallas_call(
        functools.partial(_decode_kernel, heads=heads, ts=ts),
        grid=(batch, past // tk),
        in_specs=[pl.BlockSpec((heads, ts, kc), lambda b, j: (0, b, 0)),
                  pl.BlockSpec((heads, ts, 192), lambda b, j: (0, b, 0)),
                  pl.BlockSpec((None, 1, tk, kc), lambda b, j: (layer, b, j, 0)),
                  pl.BlockSpec((None, 1, tk, 64), lambda b, j: (layer, b, j, 0)),
                  pl.BlockSpec((ts, kc), lambda b, j: (b, 0)),
                  pl.BlockSpec((ts, 64), lambda b, j: (b, 0))],
        out_specs=pl.BlockSpec((heads, ts, kc), lambda b, j: (0, b, 0)),
        out_shape=jax.ShapeDtypeStruct((heads, ms, kc), BF16),
        scratch_shapes=[pltpu.VMEM((heads * ts, 1), F32), pltpu.VMEM((heads * ts, 1), F32),
                        pltpu.VMEM((heads * ts, kc), F32)],
        compiler_params=_cparams(2),
        name="decode_attn",
    )(qlat, q, cache_ckv, cache_kr, new_ckv, new_kr)


def _olat_kernel(ol_ref, wv_ref, g_ref, o_ref):
    o = _dot(ol_ref[0], wv_ref[...].astype(BF16))
    o_ref[...] = (o * g_ref[...].astype(F32)).astype(o_ref.dtype)


def _olat(olat, w_uv, sga):
    heads, ms, kc = olat.shape
    return pl.pallas_call(
        _olat_kernel,
        grid=(heads,),
        in_specs=[pl.BlockSpec((1, ms, kc), lambda h: (h, 0, 0)),
                  pl.BlockSpec((kc, 128), lambda h: (0, h)),
                  pl.BlockSpec((ms, 128), lambda h: (0, h))],
        out_specs=pl.BlockSpec((ms, 128), lambda h: (0, h)),
        out_shape=jax.ShapeDtypeStruct((ms, heads * 128), BF16),
        compiler_params=_cparams(1),
        name="o_lat",
    )(olat, w_uv, sga)


def _pool_kernel(u_ref, halo_ref, hist_ref, g_ref, wp_ref, ps_ref, o_ref, full_ref, *, tt, pos0, group):
    i = pl.program_id(1)
    full_ref[0:POOL_HALO, :] = jnp.where(i == 0, hist_ref[0], halo_ref[0])
    full_ref[POOL_HALO:POOL_HALO + tt, :] = u_ref[0]
    pos = pos0 + i * tt + lax.broadcasted_iota(jnp.int32, (tt, 1), 0)
    for g, w in enumerate(POOL_WINDOWS):
        cols = slice(g * group, (g + 1) * group)
        win = full_ref[pl.ds(POOL_HALO, tt), cols]
        for back in range(1, w):
            win = win + full_ref[pl.ds(POOL_HALO - back, tt), cols]
        n = jnp.minimum(pos + 1, w).astype(F32)
        d = win / n - full_ref[pl.ds(POOL_HALO, tt), cols]
        y = _dot(d.astype(BF16), wp_ref[g].astype(BF16))
        y = y * ps_ref[:, cols] * g_ref[0, :, cols].astype(F32)
        o_ref[0, :, cols] = y.astype(o_ref.dtype)


def _pool_mix(u, hist16, sgb, w_pool, pool_scale, *, tt, pos0):
    batch, t, width = u.shape
    group = width // len(POOL_WINDOWS)
    hpb = tt // POOL_HALO
    return pl.pallas_call(
        functools.partial(_pool_kernel, tt=tt, pos0=pos0, group=group),
        grid=(batch, t // tt),
        in_specs=[pl.BlockSpec((1, tt, width), lambda b, i: (b, i, 0)),
                  pl.BlockSpec((1, POOL_HALO, width), lambda b, i: (b, jnp.maximum(i * hpb - 1, 0), 0)),
                  pl.BlockSpec((1, POOL_HALO, width), lambda b, i: (b, 0, 0)),
                  pl.BlockSpec((1, tt, width), lambda b, i: (b, i, 0)),
                  pl.BlockSpec(w_pool.shape, lambda b, i: (0, 0, 0)),
                  pl.BlockSpec((1, width), lambda b, i: (0, 0))],
        out_specs=pl.BlockSpec((1, tt, width), lambda b, i: (b, i, 0)),
        out_shape=jax.ShapeDtypeStruct((batch, t, width), BF16),
        scratch_shapes=[pltpu.VMEM((POOL_HALO + tt, width), F32)],
        compiler_params=_cparams(2),
        name="pool_mix",
    )(u, u, hist16, sgb, w_pool, pool_scale.reshape(1, width))


def _gmlp_kernel(uv_ref, v_ref, g_ref, wsp_ref, bsp_ref, lg_ref, lb_ref, *out_refs, tt, chunk, groups, emit_vn):
    o_ref = out_refs[0]
    width = o_ref.shape[-1]
    gw = width // groups
    vn = _ln(v_ref[...].astype(F32), lg_ref[...], lb_ref[...])
    if emit_vn:
        out_refs[1][...] = vn
    vnb = vn.astype(BF16)
    tri = (lax.broadcasted_iota(jnp.int32, (chunk, chunk), 1)
           <= lax.broadcasted_iota(jnp.int32, (chunk, chunk), 0))
    for g in range(groups):
        w = jnp.where(tri, wsp_ref[g], 0.0).astype(BF16)
        bias = bsp_ref[g]
        cols = slice(g * gw, (g + 1) * gw)
        for c in range(tt // chunk):
            rows = slice(c * chunk, (c + 1) * chunk)
            mixed = _dot(w, vnb[rows, cols]) + bias
            o_ref[rows, cols] = (uv_ref[rows, cols].astype(F32) * mixed
                                 * g_ref[rows, cols].astype(F32)).astype(o_ref.dtype)


def _gmlp(uv, sgc, w_sp, b_sp, ln_g, ln_b, *, tt, chunk, emit_vn):
    m, width = sgc.shape
    groups = w_sp.shape[0]
    out_specs = [pl.BlockSpec((tt, width), lambda i: (i, 0))]
    out_shape = [jax.ShapeDtypeStruct((m, width), BF16)]
    if emit_vn:
        out_specs.append(pl.BlockSpec((tt, width), lambda i: (i, 0)))
        out_shape.append(jax.ShapeDtypeStruct((m, width), F32))
    return pl.pallas_call(
        functools.partial(_gmlp_kernel, tt=tt, chunk=chunk, groups=groups, emit_vn=emit_vn),
        grid=(m // tt,),
        in_specs=[pl.BlockSpec((tt, width), lambda i: (i, 0)),
                  pl.BlockSpec((tt, width), lambda i: (i, 1)),
                  pl.BlockSpec((tt, width), lambda i: (i, 0)),
                  pl.BlockSpec((groups, chunk, chunk), lambda i: (0, 0, 0)),
                  pl.BlockSpec((groups, chunk, 1), lambda i: (0, 0, 0)),
                  pl.BlockSpec((1, width), lambda i: (0, 0)),
                  pl.BlockSpec((1, width), lambda i: (0, 0))],
        out_specs=out_specs,
        out_shape=out_shape,
        compiler_params=_cparams(1),
        name="gmlp",
    )(uv, uv, sgc, w_sp, b_sp, ln_g.reshape(1, width), ln_b.reshape(1, width))


def _conv_kernel(x_ref, halo_ref, hist_ref, g_ref, w_ref, b_ref, lg_ref, lb_ref, o_ref, full_ref, y_ref,
                 *, tt, taps):
    i = pl.program_id(1)
    ch = x_ref.shape[-1]
    sub = V7X_SUBLANES
    if halo_ref.shape[1] == CONV_HALO:
        head = jnp.where(i == 0, hist_ref[0], halo_ref[0])
    else:
        head = hist_ref[0]
    full_ref[0:CONV_HALO, :] = head
    full_ref[CONV_HALO:CONV_HALO + tt, :] = x_ref[0]
    first = CONV_HALO - (taps - 1)
    nv = tt // sub
    nf = (CONV_HALO + tt) // sub
    sub_idx = lax.broadcasted_iota(jnp.int32, (sub, V7X_LANES), 0)
    for c0 in range(0, ch, V7X_LANES):
        cols = slice(c0, c0 + V7X_LANES)
        f = [full_ref[j * sub:(j + 1) * sub, cols] for j in range(nf)]
        accs = [None] * nv
        for shift in range(sub):
            group = [(k, (first + k) // sub) for k in range(taps) if (first + k) % sub == shift]
            if not group:
                continue
            lo = min(a for _, a in group)
            hi = max(a for _, a in group) + nv
            if shift == 0:
                rows = {j: f[j] for j in range(lo, hi)}
            else:
                rolled = {j: pltpu.roll(f[j], sub - shift, 0) for j in range(lo, hi + 1)}
                keep = sub_idx < (sub - shift)
                rows = {j: jnp.where(keep, rolled[j], rolled[j + 1]) for j in range(lo, hi)}
            for k, a in group:
                wk = w_ref[k * sub:(k + 1) * sub, cols]
                for t in range(nv):
                    term = rows[a + t] * wk
                    accs[t] = term if accs[t] is None else accs[t] + term
        for t in range(nv):
            y_ref[t * sub:(t + 1) * sub, cols] = accs[t]
    y = _ln(y_ref[...] + b_ref[...], lg_ref[...], lb_ref[...])
    o_ref[0] = (_silu(y) * g_ref[0].astype(F32)).astype(o_ref.dtype)


def _conv_module(x, hist32, sgd, w_dw, b_dw, ln_g, ln_b, *, tt):
    batch, t, ch = x.shape
    taps = w_dw.shape[0]
    hpb = max(tt // CONV_HALO, 1)
    return pl.pallas_call(
        functools.partial(_conv_kernel, tt=tt, taps=taps),
        grid=(batch, t // tt),
        in_specs=[pl.BlockSpec((1, tt, ch), lambda b, i: (b, i, 0)),
                  pl.BlockSpec((1, min(CONV_HALO, t), ch), lambda b, i: (b, jnp.maximum(i * hpb - 1, 0), 0)),
                  pl.BlockSpec((1, CONV_HALO, ch), lambda b, i: (b, 0, 0)),
                  pl.BlockSpec((1, tt, ch), lambda b, i: (b, i, 0)),
                  pl.BlockSpec((taps * V7X_SUBLANES, ch), lambda b, i: (0, 0)),
                  pl.BlockSpec((1, ch), lambda b, i: (0, 0)),
                  pl.BlockSpec((1, ch), lambda b, i: (0, 0)),
                  pl.BlockSpec((1, ch), lambda b, i: (0, 0))],
        out_specs=pl.BlockSpec((1, tt, ch), lambda b, i: (b, i, 0)),
        out_shape=jax.ShapeDtypeStruct((batch, t, ch), BF16),
        scratch_shapes=[pltpu.VMEM((CONV_HALO + tt, ch), F32), pltpu.VMEM((tt, ch), F32)],
        compiler_params=_cparams(2),
        name="conv_module",
    )(x, x, hist32, sgd, jnp.repeat(w_dw, V7X_SUBLANES, axis=0), b_dw.reshape(1, ch),
      ln_g.reshape(1, ch), ln_b.reshape(1, ch))


def _outproj_kernel(a_ref, b_ref, y_ref, w_ref, gpost_ref, gnext_ref, ynew_ref, *h_refs):
    ka = a_ref.shape[1]
    acc = _dot(a_ref[...], w_ref[0:ka, :]) + _dot(b_ref[...], w_ref[ka:, :])
    yn = y_ref[...] + _rms(acc, gpost_ref[...])
    ynew_ref[...] = yn
    if h_refs:
        h_refs[0][...] = _rms(yn, gnext_ref[...]).astype(h_refs[0].dtype)


def _outproj(a, b, y, w_bf, layer, g_post, g_next, *, tm):
    m, d = y.shape
    ka, kb = a.shape[1], b.shape[1]
    emit_h = g_next is not None
    gn = g_next if emit_h else g_post
    out_specs = [pl.BlockSpec((tm, d), lambda i: (i, 0))]
    out_shape = [jax.ShapeDtypeStruct((m, d), F32)]
    if emit_h:
        out_specs.append(pl.BlockSpec((tm, d), lambda i: (i, 0)))
        out_shape.append(jax.ShapeDtypeStruct((m, d), BF16))
    outs = pl.pallas_call(
        _outproj_kernel,
        grid=(m // tm,),
        in_specs=[pl.BlockSpec((tm, ka), lambda i: (i, 0)),
                  pl.BlockSpec((tm, kb), lambda i: (i, 0)),
                  pl.BlockSpec((tm, d), lambda i: (i, 0)),
                  pl.BlockSpec((None, ka + kb, d), lambda i: (layer, 0, 0)),
                  pl.BlockSpec((1, d), lambda i: (0, 0)),
                  pl.BlockSpec((1, d), lambda i: (0, 0))],
        out_specs=out_specs,
        out_shape=out_shape,
        compiler_params=_cparams(1),
        name="out_proj",
    )(a, b, y, w_bf, g_post.reshape(1, d), gn.reshape(1, d))
    return (outs[0], outs[1]) if emit_h else (outs[0], None)


def _rope_table(pos, half):
    inv = jnp.power(jnp.float32(ROPE_THETA), -jnp.arange(half, dtype=jnp.float32) / half)
    ang = pos.astype(jnp.float32)[:, None] * inv[None, :]
    cos, sin = jnp.cos(ang), jnp.sin(ang)
    return jnp.concatenate([cos, cos, -sin, sin], axis=-1)


def _swap_halves(w):
    half = w.shape[-1] // 2
    return jnp.concatenate([w[..., half:], w[..., :half]], axis=-1)


def _even_weights(w_in, w_uq, w_uk, w_uv, dims):
    q_lora, kv_lora, qk_rope = dims
    kr0 = q_lora + kv_lora
    w_kr = w_in[:, kr0:kr0 + qk_rope]
    w_kr_ext = jnp.concatenate([w_kr, _swap_halves(w_kr)], axis=1)
    nope = w_uq.shape[-1] - qk_rope
    wq_ext = jnp.concatenate([w_uq, _swap_halves(w_uq[..., nope:])], axis=-1)
    wq_ext = wq_ext.reshape(q_lora, -1)
    return w_kr_ext, wq_ext, w_uk.reshape(kv_lora, -1), w_uv.reshape(kv_lora, -1)


def _even_layer(h, y, tab, batch, t, past_kv, pool_hist16, wts, layer, g_post, g_next, *, pos0, tm, sm_scale):
    (w_in, g_q, g_kv, w_kr_ext, wq_ext, w_uk2, w_uv2, w_pool, pool_scale, w_out_bf) = wts
    m = h.shape[0]
    q_lora, kv_lora = g_q.shape[0], g_kv.shape[0]
    mla_width = w_uv2.shape[1]
    pool_width = pool_scale.shape[0]
    nt = tab.shape[0] // tm
    rest0 = q_lora + kv_lora + 64

    cq = _simple_proj(h, w_in, layer, 0, q_lora, q_lora, _epi_rms(BF16), BF16, tm=tm, name="in_cq",
                      extras=[(g_q.reshape(1, q_lora), (1, q_lora), lambda j, i: (0, 0))])
    ckv, kr = _proj(h, [(w_in, layer, kv_lora, q_lora), (w_kr_ext, None, 128, 0)],
                    [(g_kv.reshape(1, kv_lora), (1, kv_lora), lambda j, i: (0, 0)),
                     (tab, (tm, 128), lambda j, i: (i % nt, 0))],
                    [((m, kv_lora), F32, (tm, kv_lora), lambda j, i: (i, 0)),
                     ((m, 64), F32, (tm, 64), lambda j, i: (i, 0))],
                    _epi_kv, tm=tm, nj=1, name="in_ckv")
    tn = 1024
    sga = _simple_proj(h, w_in, layer, rest0, mla_width, tn, _epi_silu, BF16, tm=tm, name="in_ga")
    u_b = _simple_proj(h, w_in, layer, rest0 + mla_width, pool_width, tn, _epi_copy, F32, tm=tm, name="in_ub")
    sgb = _simple_proj(h, w_in, layer, rest0 + mla_width + pool_width, pool_width, tn, _epi_silu, BF16,
                       tm=tm, name="in_gb")

    q = _qup(cq, wq_ext, tab, tm=min(tm, 512), sm_scale=sm_scale)
    if past_kv is None:
        k, vt = _kvup(ckv, kr, w_uk2, w_uv2.T, tm=min(tm, 512))
        a = _attn_prompt(q, k, vt, sga, batch=batch, seq=t, hb=8, tq=256)
    else:
        past_ckv, past_kr = past_kv
        assert past_ckv.shape[2] % CHUNK == 0 and t <= CHUNK
        qlat = _qlat(q, w_uk2)
        olat = _decode_attn(qlat, q, past_ckv, past_kr, layer, ckv, kr, ts=t, tk=min(1024, past_ckv.shape[2]))
        a = _olat(olat, w_uv2, sga)

    u3 = u_b.reshape(batch, t, pool_width)
    b = _pool_mix(u3, pool_hist16, sgb.reshape(batch, t, pool_width), w_pool, pool_scale,
                  tt=min(t, 256), pos0=pos0).reshape(m, pool_width)
    y_new, h_next = _outproj(a, b, y, w_out_bf, layer, g_post, g_next, tm=min(tm, 256))
    return y_new, h_next, ckv, kr, u3


def _odd_layer(h, y, batch, t, conv_hist32, wts, layer, g_post, g_next, *, tm, emit_vn):
    (w_in, ln_v_g, ln_v_b, w_sp, b_sp, w_dw, b_dw, ln_c_g, ln_c_b, w_out_bf) = wts
    m = h.shape[0]
    width = ln_v_g.shape[0]
    ch = ln_c_g.shape[0]
    tn = 1024
    uv = _simple_proj(h, w_in, layer, 0, 2 * width, tn, _epi_copy, BF16, tm=tm, name="in_uv")
    sgc = _simple_proj(h, w_in, layer, 2 * width, width, tn, _epi_silu, BF16, tm=tm, name="in_gc")
    tg = 512
    x_c = _proj(h, [(w_in, layer, tg, 3 * width), (w_in, layer, tg, 3 * width + ch)], [],
                [((m, ch), F32, (tm, tg), lambda j, i: (i, j))],
                _epi_glu, tm=tm, nj=ch // tg, name="in_glu")[0]
    sgd = _simple_proj(h, w_in, layer, 3 * width + 2 * ch, ch, tn, _epi_silu, BF16, tm=tm, name="in_gd")

    chunk = w_sp.shape[-1]
    gm = _gmlp(uv, sgc, w_sp, b_sp, ln_v_g, ln_v_b, tt=min(m, 512), chunk=chunk, emit_vn=emit_vn)
    c, vn = (gm[0], gm[1]) if emit_vn else (gm[0], None)
    x3 = x_c.reshape(batch, t, ch)
    yc = _conv_module(x3, conv_hist32, sgd.reshape(batch, t, ch), w_dw, b_dw, ln_c_g, ln_c_b,
                      tt=min(t, 128)).reshape(m, ch)
    y_new, h_next = _outproj(c, yc, y, w_out_bf, layer, g_post, g_next, tm=min(tm, 256))
    return y_new, h_next, x3, vn


def kernel(x_prompt, x_sample, cache_mla_ckv, cache_mla_krope, state_pool, state_conv, ln_pre, ln_post,
           w_in_even, g_q_lat, g_kv_lat, w_uq, w_uk, w_uv, w_pool, pool_scale, w_out_even, w_in_odd,
           ln_v_g, ln_v_b, w_sp, b_sp, w_dw, b_dw, ln_c_g, ln_c_b, w_out_odd):
    bp, sp, d = x_prompt.shape
    bs, ts, _ = x_sample.shape
    past = cache_mla_ckv.shape[2]
    depth = ln_pre.shape[0]
    q_lora, kv_lora = g_q_lat.shape[1], g_kv_lat.shape[1]
    qk_rope = cache_mla_krope.shape[-1]
    qk_nope = w_uq.shape[-1] - qk_rope
    mla_width = w_uv.shape[2] * w_uv.shape[3]
    pool_width = pool_scale.shape[1]
    pool_hist = state_pool.shape[2]
    conv_hist = state_conv.shape[2]
    ch = state_conv.shape[3]
    sm_scale = float((qk_nope + qk_rope) ** -0.5 * math.log2(math.e))
    mp, ms = bp * sp, bs * ts
    tmp, tms = min(1024, sp), ms

    tab_p = _rope_table(jnp.arange(sp), qk_rope // 2)
    tab_s = jnp.tile(_rope_table(past + jnp.arange(ts), qk_rope // 2), (bs, 1))

    yp = x_prompt.reshape(mp, d)
    ys = x_sample.reshape(ms, d)
    hp = _rms_cast(yp, ln_pre[0], tmp)
    hs = _rms_cast(ys, ln_pre[0], tms)

    w_out_even_bf = w_out_even.astype(BF16)
    w_out_odd_bf = w_out_odd.astype(BF16)
    ckv_p, kr_p, pool_p, conv_p = [], [], [], []
    ckv_s, kr_s, pool_s, conv_s, v_s = [], [], [], [], []
    for layer in range(depth):
        i = layer // 2
        g_post = ln_post[layer]
        g_next = ln_pre[layer + 1] if layer + 1 < depth else None
        if layer % 2 == 0:
            w_kr_ext, wq_ext, w_uk2, w_uv2 = _even_weights(
                w_in_even[i], w_uq[i], w_uk[i], w_uv[i], (q_lora, kv_lora, qk_rope))
            wts = (w_in_even, g_q_lat[i], g_kv_lat[i], w_kr_ext, wq_ext, w_uk2, w_uv2,
                   w_pool[i], pool_scale[i], w_out_even_bf)
            zero_hist = jnp.zeros((bp, POOL_HALO, pool_width), F32)
            yp, hp, c1, k1, u1 = _even_layer(hp, yp, tab_p, bp, sp, None, zero_hist, wts, i, g_post, g_next,
                                             pos0=0, tm=tmp, sm_scale=sm_scale)
            hist16 = jnp.pad(state_pool[i], ((0, 0), (POOL_HALO - pool_hist, 0), (0, 0)))
            ys, hs, c2, k2, u2 = _even_layer(hs, ys, tab_s, bs, ts, (cache_mla_ckv, cache_mla_krope),
                                             hist16, wts, i, g_post, g_next, pos0=past, tm=tms,
                                             sm_scale=sm_scale)
            ckv_p.append(c1.reshape(bp, sp, kv_lora))
            kr_p.append(k1.reshape(bp, sp, qk_rope))
            pool_p.append(u1[:, sp - pool_hist:])
            ckv_s.append(c2.reshape(bs, ts, kv_lora))
            kr_s.append(k2.reshape(bs, ts, qk_rope))
            pool_s.append(jnp.concatenate([state_pool[i], u2], axis=1)[:, -pool_hist:])
        else:
            wts = (w_in_odd, ln_v_g[i], ln_v_b[i], w_sp[i], b_sp[i][:, :, None], w_dw[i], b_dw[i],
                   ln_c_g[i], ln_c_b[i], w_out_odd_bf)
            zero_hist = jnp.zeros((bp, CONV_HALO, ch), F32)
            yp, hp, x1, _ = _odd_layer(hp, yp, bp, sp, zero_hist, wts, i, g_post, g_next, tm=tmp, emit_vn=False)
            lc = min(ts, w_sp.shape[-1])
            wts_s = wts[:3] + (w_sp[i][:, :lc, :lc], b_sp[i][:, :lc, None]) + wts[5:]
            hist32 = jnp.pad(state_conv[i], ((0, 0), (CONV_HALO - conv_hist, 0), (0, 0)))
            ys, hs, x2, v2 = _odd_layer(hs, ys, bs, ts, hist32, wts_s, i, g_post, g_next, tm=tms, emit_vn=True)
            conv_p.append(x1[:, sp - conv_hist:])
            conv_s.append(jnp.concatenate([state_conv[i], x2], axis=1)[:, -conv_hist:])
            v_s.append(v2.reshape(bs, ts, -1))

    return (yp.reshape(bp, sp, d), ys.reshape(bs, ts, d),
            jnp.stack(ckv_p), jnp.stack(kr_p), jnp.stack(pool_p), jnp.stack(conv_p),
            jnp.stack(ckv_s), jnp.stack(kr_s), jnp.stack(pool_s), jnp.stack(conv_s), jnp.stack(v_s))
```

```python
import functools
import math

import jax
import jax.numpy as jnp
from jax import lax
from jax.experimental import pallas as pl
from jax.experimental.pallas import tpu as pltpu

F32 = jnp.float32
BF16 = jnp.bfloat16

EPS = 1e-6
NEG_INF = -1e30
CHUNK = 64
ROPE_THETA = 10000.0
POOL_WINDOWS = (2, 4, 8, 16)
POOL_HALO = 16
CONV_HALO = 32
ATTN_SCORE_LOOKAHEAD = 4

V7X_LANES = 128
V7X_SUBLANES = 8
V7X_VMEM_LIMIT_BYTES = 56 * 1024 * 1024


def _cparams(ndims):
    return pltpu.CompilerParams(dimension_semantics=("arbitrary",) * ndims,
                                vmem_limit_bytes=V7X_VMEM_LIMIT_BYTES)


def _rms(x, g):
    return x * lax.rsqrt(jnp.mean(x * x, axis=-1, keepdims=True) + EPS) * g


def _ln(x, g, b):
    mu = jnp.mean(x, axis=-1, keepdims=True)
    xc = x - mu
    var = jnp.mean(xc * xc, axis=-1, keepdims=True)
    return xc * lax.rsqrt(var + EPS) * g + b


def _silu(x):
    return x * jax.nn.sigmoid(x)


def _dot(a, b):
    return jnp.dot(a, b, preferred_element_type=F32)


def _rms_cast_kernel(x_ref, g_ref, o_ref):
    o_ref[...] = _rms(x_ref[...], g_ref[...]).astype(o_ref.dtype)


def _rms_cast(x, g, tm):
    m, d = x.shape
    return pl.pallas_call(
        _rms_cast_kernel,
        grid=(m // tm,),
        in_specs=[pl.BlockSpec((tm, d), lambda i: (i, 0)),
                  pl.BlockSpec((1, d), lambda i: (0, 0))],
        out_specs=pl.BlockSpec((tm, d), lambda i: (i, 0)),
        out_shape=jax.ShapeDtypeStruct((m, d), BF16),
        compiler_params=_cparams(1),
        name="rms_cast",
    )(x, g.reshape(1, d))


def _w_block_spec(w, layer, k, width, block_of_j):
    if layer is None:
        return pl.BlockSpec((k, width), lambda j, i: (0, block_of_j(j)))
    return pl.BlockSpec((None, k, width), lambda j, i: (layer, 0, block_of_j(j)))


def _wt_block_spec(w, layer, rows, k, block_of_j):
    if layer is None:
        return pl.BlockSpec((rows, k), lambda j, i: (block_of_j(j), 0))
    return pl.BlockSpec((None, rows, k), lambda j, i: (layer, block_of_j(j), 0))


def _proj(a, w_specs, extras, out_defs, epilogue, *, tm, nj, name):
    m, k = a.shape
    nw, ne, no = len(w_specs), len(extras), len(out_defs)
    in_specs = [pl.BlockSpec((tm, k), lambda j, i: (i, 0))]
    w_args, tile_plans, scratch = [], [], []
    for w, layer, tn, col0, transposed in w_specs:
        off = col0 % (tn if transposed else V7X_LANES)
        base = col0 - off
        assert base % tn == 0 and tn % V7X_LANES == 0
        main_of_j = functools.partial(lambda j, b: j + b, b=base // tn)
        if transposed:
            in_specs.append(_wt_block_spec(w, layer, tn, k, main_of_j))
        else:
            in_specs.append(_w_block_spec(w, layer, k, tn, main_of_j))
        w_args.append(w)
        if off:
            extra = off if transposed else V7X_LANES
            assert extra % 16 == 0 and (base + tn) % extra == 0 and tn % extra == 0
            extra_of_j = functools.partial(lambda j, b, s: j * s + b, b=(base + tn) // extra, s=tn // extra)
            if transposed:
                in_specs.append(_wt_block_spec(w, layer, extra, k, extra_of_j))
            else:
                in_specs.append(_w_block_spec(w, layer, k, extra, extra_of_j))
            w_args.append(w)
        tile_plans.append((off, transposed))
        scratch.append(pltpu.VMEM((tn, k) if transposed else (k, tn), BF16))
    nwin = len(w_args)

    def body(*refs):
        a_ref = refs[0]
        w_refs = refs[1:1 + nwin]
        e_refs = refs[1 + nwin:1 + nwin + ne]
        o_refs = refs[1 + nwin + ne:1 + nwin + ne + no]
        s_refs = refs[1 + nwin + ne + no:]

        @pl.when(pl.program_id(1) == 0)
        def _():
            w_iter = iter(w_refs)
            for (off, transposed), s_ref in zip(tile_plans, s_refs):
                tile = next(w_iter)[...].astype(BF16)
                if off and transposed:
                    tile = jnp.concatenate([tile[off:], next(w_iter)[...].astype(BF16)], axis=0)
                elif off:
                    tile = jnp.concatenate([tile, next(w_iter)[...].astype(BF16)], axis=1)
                    tile = tile[:, off:off + s_ref.shape[1]]
                s_ref[...] = tile

        av = a_ref[...].astype(BF16)
        accs = []
        for (_, transposed), s_ref in zip(tile_plans, s_refs):
            if transposed:
                accs.append(lax.dot_general(av, s_ref[...], (((1,), (1,)), ((), ())),
                                            preferred_element_type=F32))
            else:
                accs.append(_dot(av, s_ref[...]))
        epilogue(accs, e_refs, o_refs)

    for arr, bs, im in extras:
        in_specs.append(pl.BlockSpec(bs, im))
    out_specs = [pl.BlockSpec(bs, im) for (_, _, bs, im) in out_defs]
    out_shape = [jax.ShapeDtypeStruct(sh, dt) for (sh, dt, _, _) in out_defs]
    outs = pl.pallas_call(
        body,
        grid=(nj, m // tm),
        in_specs=in_specs,
        out_specs=out_specs,
        out_shape=out_shape,
        scratch_shapes=scratch,
        compiler_params=_cparams(2),
        name=name,
    )(a, *w_args, *[e for (e, _, _) in extras])
    return outs


def _epi_rms(out_dtype):
    def epi(accs, e_refs, o_refs):
        o_refs[0][...] = _rms(accs[0], e_refs[0][...]).astype(out_dtype)
    return epi


def _epi_silu(accs, e_refs, o_refs):
    o_refs[0][...] = _silu(accs[0]).astype(o_refs[0].dtype)


def _epi_copy(accs, e_refs, o_refs):
    o_refs[0][...] = accs[0].astype(o_refs[0].dtype)


def _epi_glu(accs, e_refs, o_refs):
    o_refs[0][...] = (accs[0] * jax.nn.sigmoid(accs[1])).astype(o_refs[0].dtype)


def _rope_pair(t):
    return t[:, :64] + t[:, 64:]


def _epi_kv(accs, e_refs, o_refs):
    o_refs[0][...] = _rms(accs[0], e_refs[0][...])
    o_refs[1][...] = _rope_pair(accs[1] * e_refs[1][...])


def _simple_proj(a, w, layer, col0, ncols, tn, epilogue, out_dtype, *, tm, name, extras=(), transposed=False):
    m = a.shape[0]
    assert ncols % tn == 0
    return _proj(a, [(w, layer, tn, col0, transposed)], list(extras),
                 [((m, ncols), out_dtype, (tm, tn), lambda j, i: (i, j))],
                 epilogue, tm=tm, nj=ncols // tn, name=name)[0]


def _qup_kernel(cq_ref, w_ref, tab_ref, q_ref, wbf_ref, *, heads, sm_scale):
    @pl.when(pl.program_id(0) == 0)
    def _():
        wbf_ref[...] = w_ref[...].astype(BF16)

    acc = _dot(cq_ref[...], wbf_ref[...])
    tab = tab_ref[...] * sm_scale
    for h in range(heads):
        base = h * 256
        q_ref[h, :, 0:128] = (acc[:, base:base + 128] * sm_scale).astype(q_ref.dtype)
        q_ref[h, :, 128:192] = _rope_pair(acc[:, base + 128:base + 256] * tab).astype(q_ref.dtype)


def _qup(cq, wq_ext, tab, *, tm, sm_scale):
    m, kq = cq.shape
    heads = wq_ext.shape[1] // 256
    nt = tab.shape[0] // tm
    return pl.pallas_call(
        functools.partial(_qup_kernel, heads=heads, sm_scale=sm_scale),
        grid=(m // tm,),
        in_specs=[pl.BlockSpec((tm, kq), lambda i: (i, 0)),
                  pl.BlockSpec(wq_ext.shape, lambda i: (0, 0)),
                  pl.BlockSpec((tm, 128), lambda i: (i % nt, 0))],
        out_specs=pl.BlockSpec((heads, tm, 192), lambda i: (0, i, 0)),
        out_shape=jax.ShapeDtypeStruct((heads, m, 192), BF16),
        scratch_shapes=[pltpu.VMEM(wq_ext.shape, BF16)],
        compiler_params=_cparams(1),
        name="q_up",
    )(cq, wq_ext, tab)


def _kvup_kernel(ckv_ref, kr_ref, wk_ref, wv_ref, k_ref, v_ref, wkbf_ref, wvbf_ref, *, heads):
    @pl.when(pl.program_id(0) == 0)
    def _():
        wkbf_ref[...] = wk_ref[...].astype(BF16)
        wvbf_ref[...] = wv_ref[...].astype(BF16)

    c = ckv_ref[...].astype(BF16)
    kn = _dot(c, wkbf_ref[...]).astype(k_ref.dtype)
    v_ref[...] = lax.dot_general(wvbf_ref[...], c, (((1,), (1,)), ((), ())),
                                 preferred_element_type=F32).astype(v_ref.dtype)
    kr = kr_ref[...].astype(k_ref.dtype)
    for h in range(heads):
        k_ref[h, :, 0:128] = kn[:, h * 128:(h + 1) * 128]
        k_ref[h, :, 128:192] = kr


def _kvup(ckv, kr, w_uk, w_uv_t, *, tm):
    m, kc = ckv.shape
    heads = w_uk.shape[1] // 128
    return pl.pallas_call(
        functools.partial(_kvup_kernel, heads=heads),
        grid=(m // tm,),
        in_specs=[pl.BlockSpec((tm, kc), lambda i: (i, 0)),
                  pl.BlockSpec((tm, 64), lambda i: (i, 0)),
                  pl.BlockSpec(w_uk.shape, lambda i: (0, 0)),
                  pl.BlockSpec(w_uv_t.shape, lambda i: (0, 0))],
        out_specs=[pl.BlockSpec((heads, tm, 192), lambda i: (0, i, 0)),
                   pl.BlockSpec((heads * 128, tm), lambda i: (0, i))],
        out_shape=[jax.ShapeDtypeStruct((heads, m, 192), BF16),
                   jax.ShapeDtypeStruct((heads * 128, m), BF16)],
        scratch_shapes=[pltpu.VMEM(w_uk.shape, BF16), pltpu.VMEM(w_uv_t.shape, BF16)],
        compiler_params=_cparams(1),
        name="kv_up",
    )(ckv, kr, w_uk, w_uv_t)


def _attn_kernel(q_ref, k_ref, vt_ref, g_ref, o_ref, m_ref, l_ref, acc_ref, st_ref, *, hb, tq):
    qi = pl.program_id(2)
    key_chunk = lax.broadcasted_iota(jnp.int32, (tq, tq), 0) // CHUNK
    qry_chunk = lax.broadcasted_iota(jnp.int32, (tq, tq), 1) // CHUNK
    diag_mask = key_chunk <= qry_chunk

    m_ref[...] = jnp.full(m_ref.shape, NEG_INF, F32)
    l_ref[...] = jnp.zeros(l_ref.shape, F32)
    acc_ref[...] = jnp.zeros(acc_ref.shape, F32)

    ahead = st_ref.shape[0]

    def scores(kb, hh):
        k = k_ref[hh, pl.ds(pl.multiple_of(kb * tq, tq), tq), :]
        return lax.dot_general(k, q_ref[hh], (((1,), (1,)), ((), ())), preferred_element_type=F32)

    def block(kb, masked):
        start = pl.multiple_of(kb * tq, tq)

        def softmax(hh, st):
            if masked:
                st = jnp.where(diag_mask, st, NEG_INF)
            m_old = m_ref[hh]
            m_new = jnp.maximum(m_old, jnp.max(st, axis=0, keepdims=True))
            alpha = jnp.exp2(m_old - m_new)
            p = jnp.exp2(st - m_new)
            l_ref[hh] = alpha * l_ref[hh] + jnp.sum(p, axis=0, keepdims=True)
            m_ref[hh] = m_new
            return alpha, p.astype(BF16)

        def accumulate(hh, alpha, p):
            vt = vt_ref[hh * 128:(hh + 1) * 128, pl.ds(start, tq)]
            acc_ref[hh] = alpha * acc_ref[hh] + _dot(vt, p)

        st = {}
        pending = None
        for hh in range(hb):
            cur = st_ref[hh] if hh < ahead else st.pop(hh)
            nxt = hh + ahead
            if nxt < hb:
                st[nxt] = scores(kb, nxt)
            elif not masked:
                st_ref[nxt - hb] = scores(kb + 1, nxt - hb)
            alpha, p = softmax(hh, cur)
            if pending is not None:
                accumulate(*pending)
            pending = (hh, alpha, p)
        accumulate(*pending)

    def body(kb, carry):
        block(kb, False)
        return carry

    for hh in range(ahead):
        st_ref[hh] = scores(0, hh)
    lax.fori_loop(0, qi, body, 0)
    block(qi, True)
    for hh in range(hb):
        o = (acc_ref[hh] / l_ref[hh]).T
        gate = g_ref[:, hh * 128:(hh + 1) * 128].astype(F32)
        o_ref[:, hh * 128:(hh + 1) * 128] = (o * gate).astype(o_ref.dtype)


def _attn_prompt(q, k, vt, sga, *, batch, seq, hb, tq):
    heads, m, _ = q.shape
    nq = seq // tq
    return pl.pallas_call(
        functools.partial(_attn_kernel, hb=hb, tq=tq),
        grid=(batch, heads // hb, nq),
        in_specs=[pl.BlockSpec((hb, tq, 192), lambda b, g, i: (g, b * nq + i, 0)),
                  pl.BlockSpec((hb, seq, 192), lambda b, g, i: (g, b, 0)),
                  pl.BlockSpec((hb * 128, seq), lambda b, g, i: (g, b)),
                  pl.BlockSpec((tq, hb * 128), lambda b, g, i: (b * nq + i, g))],
        out_specs=pl.BlockSpec((tq, hb * 128), lambda b, g, i: (b * nq + i, g)),
        out_shape=jax.ShapeDtypeStruct((m, heads * 128), BF16),
        scratch_shapes=[pltpu.VMEM((hb, 1, tq), F32), pltpu.VMEM((hb, 1, tq), F32),
                        pltpu.VMEM((hb, 128, tq), F32),
                        pltpu.VMEM((min(ATTN_SCORE_LOOKAHEAD, hb - 1), tq, tq), F32)],
        compiler_params=_cparams(3),
        name="attn_prompt",
    )(q, k, vt, sga)


def _qlat_kernel(q_ref, wk_ref, o_ref):
    qn = q_ref[0, :, 0:128]
    o_ref[0] = lax.dot_general(qn, wk_ref[...].astype(BF16), (((1,), (1,)), ((), ())),
                               preferred_element_type=F32).astype(o_ref.dtype)


def _qlat(q, w_uk):
    heads, m, _ = q.shape
    kc = w_uk.shape[0]
    return pl.pallas_call(
        _qlat_kernel,
        grid=(heads,),
        in_specs=[pl.BlockSpec((1, m, 192), lambda h: (h, 0, 0)),
                  pl.BlockSpec((kc, 128), lambda h: (0, h))],
        out_specs=pl.BlockSpec((1, m, kc), lambda h: (h, 0, 0)),
        out_shape=jax.ShapeDtypeStruct((heads, m, kc), BF16),
        compiler_params=_cparams(1),
        name="q_lat",
    )(q, w_uk)


def _decode_kernel(ql_ref, q_ref, ckv_ref, kr_ref, nckv_ref, nkr_ref, o_ref, m_ref, l_ref, acc_ref, *, heads, ts):
    kb = pl.program_id(1)
    rows = heads * ts

    @pl.when(kb == 0)
    def _():
        m_ref[...] = jnp.full(m_ref.shape, NEG_INF, F32)
        l_ref[...] = jnp.zeros(l_ref.shape, F32)
        acc_ref[...] = jnp.zeros(acc_ref.shape, F32)

    ql = ql_ref[...].reshape(rows, ql_ref.shape[-1])
    qr = q_ref[:, :, 128:192].reshape(rows, 64)

    def step(ckv, kr, kr_is_transposed):
        s_rope = (_dot(qr, kr) if kr_is_transposed else
                  lax.dot_general(qr, kr, (((1,), (1,)), ((), ())), preferred_element_type=F32))
        s = lax.dot_general(ql, ckv, (((1,), (1,)), ((), ())), preferred_element_type=F32) + s_rope
        m_old = m_ref[...]
        m_new = jnp.maximum(m_old, jnp.max(s, axis=-1, keepdims=True))
        alpha = jnp.exp2(m_old - m_new)
        p = jnp.exp2(s - m_new)
        l_ref[...] = alpha * l_ref[...] + jnp.sum(p, axis=-1, keepdims=True)
        acc_ref[...] = alpha * acc_ref[...] + _dot(p.astype(BF16), ckv)
        m_ref[...] = m_new

    step(ckv_ref[0].astype(BF16), kr_ref[0].astype(BF16), True)

    @pl.when(kb == pl.num_programs(1) - 1)
    def _():
        step(nckv_ref[...].astype(BF16), nkr_ref[...].astype(BF16), False)
        o = acc_ref[...] / l_ref[...]
        o_ref[...] = o.reshape(o_ref.shape).astype(o_ref.dtype)


def _decode_attn(qlat, q, cache_ckv, cache_kr, layer, new_ckv, new_kr, *, ts, tk):
    heads, ms, kc = qlat.shape
    _, batch, past, _ = cache_ckv.shape
    return pl.pallas_call(
        functools.partial(_decode_kernel, heads=heads, ts=ts),
        grid=(batch, past // tk),
        in_specs=[pl.BlockSpec((heads, ts, kc), lambda b, j: (0, b, 0)),
                  pl.BlockSpec((heads, ts, 192), lambda b, j: (0, b, 0)),
                  pl.BlockSpec((None, 1, tk, kc), lambda b, j: (layer, b, j, 0)),
                  pl.BlockSpec((None, 1, 64, tk), lambda b, j: (layer, b, 0, j)),
                  pl.BlockSpec((ts, kc), lambda b, j: (b, 0)),
                  pl.BlockSpec((ts, 64), lambda b, j: (b, 0))],
        out_specs=pl.BlockSpec((heads, ts, kc), lambda b, j: (0, b, 0)),
        out_shape=jax.ShapeDtypeStruct((heads, ms, kc), BF16),
        scratch_shapes=[pltpu.VMEM((heads * ts, 1), F32), pltpu.VMEM((heads * ts, 1), F32),
                        pltpu.VMEM((heads * ts, kc), F32)],
        compiler_params=_cparams(2),
        name="decode_attn",
    )(qlat, q, cache_ckv, cache_kr, new_ckv, new_kr)


def _olat_kernel(ol_ref, wv_ref, g_ref, o_ref):
    o = _dot(ol_ref[0], wv_ref[...].astype(BF16))
    o_ref[...] = (o * g_ref[...].astype(F32)).astype(o_ref.dtype)


def _olat(olat, w_uv, sga):
    heads, ms, kc = olat.shape
    return pl.pallas_call(
        _olat_kernel,
        grid=(heads,),
        in_specs=[pl.BlockSpec((1, ms, kc), lambda h: (h, 0, 0)),
                  pl.BlockSpec((kc, 128), lambda h: (0, h)),
                  pl.BlockSpec((ms, 128), lambda h: (0, h))],
        out_specs=pl.BlockSpec((ms, 128), lambda h: (0, h)),
        out_shape=jax.ShapeDtypeStruct((ms, heads * 128), BF16),
        compiler_params=_cparams(1),
        name="o_lat",
    )(olat, w_uv, sga)


def _pool_kernel(u_ref, halo_ref, hist_ref, g_ref, wp_ref, ps_ref, o_ref, full_ref, *, tt, pos0, group):
    i = pl.program_id(1)
    full_ref[0:POOL_HALO, :] = jnp.where(i == 0, hist_ref[0], halo_ref[0])
    full_ref[POOL_HALO:POOL_HALO + tt, :] = u_ref[0]
    pos = pos0 + i * tt + lax.broadcasted_iota(jnp.int32, (tt, 1), 0)
    for g, w in enumerate(POOL_WINDOWS):
        cols = slice(g * group, (g + 1) * group)
        win = full_ref[pl.ds(POOL_HALO, tt), cols]
        for back in range(1, w):
            win = win + full_ref[pl.ds(POOL_HALO - back, tt), cols]
        n = jnp.minimum(pos + 1, w).astype(F32)
        d = win / n - full_ref[pl.ds(POOL_HALO, tt), cols]
        y = _dot(d.astype(BF16), wp_ref[g].astype(BF16))
        y = y * ps_ref[:, cols] * g_ref[0, :, cols].astype(F32)
        o_ref[0, :, cols] = y.astype(o_ref.dtype)


def _pool_mix(u, hist16, sgb, w_pool, pool_scale, *, tt, pos0):
    batch, t, width = u.shape
    group = width // len(POOL_WINDOWS)
    hpb = tt // POOL_HALO
    return pl.pallas_call(
        functools.partial(_pool_kernel, tt=tt, pos0=pos0, group=group),
        grid=(batch, t // tt),
        in_specs=[pl.BlockSpec((1, tt, width), lambda b, i: (b, i, 0)),
                  pl.BlockSpec((1, POOL_HALO, width), lambda b, i: (b, jnp.maximum(i * hpb - 1, 0), 0)),
                  pl.BlockSpec((1, POOL_HALO, width), lambda b, i: (b, 0, 0)),
                  pl.BlockSpec((1, tt, width), lambda b, i: (b, i, 0)),
                  pl.BlockSpec(w_pool.shape, lambda b, i: (0, 0, 0)),
                  pl.BlockSpec((1, width), lambda b, i: (0, 0))],
        out_specs=pl.BlockSpec((1, tt, width), lambda b, i: (b, i, 0)),
        out_shape=jax.ShapeDtypeStruct((batch, t, width), BF16),
        scratch_shapes=[pltpu.VMEM((POOL_HALO + tt, width), F32)],
        compiler_params=_cparams(2),
        name="pool_mix",
    )(u, u, hist16, sgb, w_pool, pool_scale.reshape(1, width))


def _gmlp_kernel(uv_ref, v_ref, g_ref, wsp_ref, bsp_ref, lg_ref, lb_ref, *out_refs, tt, chunk, groups, emit_vn):
    o_ref = out_refs[0]
    width = o_ref.shape[-1]
    gw = width // groups
    vn = _ln(v_ref[...].astype(F32), lg_ref[...], lb_ref[...])
    if emit_vn:
        out_refs[1][...] = vn
    vnb = vn.astype(BF16)
    tri = (lax.broadcasted_iota(jnp.int32, (chunk, chunk), 1)
           <= lax.broadcasted_iota(jnp.int32, (chunk, chunk), 0))
    for g in range(groups):
        w = jnp.where(tri, wsp_ref[g], 0.0).astype(BF16)
        bias = bsp_ref[g]
        cols = slice(g * gw, (g + 1) * gw)
        for c in range(tt // chunk):
            rows = slice(c * chunk, (c + 1) * chunk)
            mixed = _dot(w, vnb[rows, cols]) + bias
            o_ref[rows, cols] = (uv_ref[rows, cols].astype(F32) * mixed
                                 * g_ref[rows, cols].astype(F32)).astype(o_ref.dtype)


def _gmlp(uv, sgc, w_sp, b_sp, ln_g, ln_b, *, tt, chunk, emit_vn):
    m, width = sgc.shape
    groups = w_sp.shape[0]
    out_specs = [pl.BlockSpec((tt, width), lambda i: (i, 0))]
    out_shape = [jax.ShapeDtypeStruct((m, width), BF16)]
    if emit_vn:
        out_specs.append(pl.BlockSpec((tt, width), lambda i: (i, 0)))
        out_shape.append(jax.ShapeDtypeStruct((m, width), F32))
    return pl.pallas_call(
        functools.partial(_gmlp_kernel, tt=tt, chunk=chunk, groups=groups, emit_vn=emit_vn),
        grid=(m // tt,),
        in_specs=[pl.BlockSpec((tt, width), lambda i: (i, 0)),
                  pl.BlockSpec((tt, width), lambda i: (i, 1)),
                  pl.BlockSpec((tt, width), lambda i: (i, 0)),
                  pl.BlockSpec((groups, chunk, chunk), lambda i: (0, 0, 0)),
                  pl.BlockSpec((groups, chunk, 1), lambda i: (0, 0, 0)),
                  pl.BlockSpec((1, width), lambda i: (0, 0)),
                  pl.BlockSpec((1, width), lambda i: (0, 0))],
        out_specs=out_specs,
        out_shape=out_shape,
        compiler_params=_cparams(1),
        name="gmlp",
    )(uv, uv, sgc, w_sp, b_sp, ln_g.reshape(1, width), ln_b.reshape(1, width))


def _conv_kernel(x_ref, halo_ref, hist_ref, g_ref, w_ref, b_ref, lg_ref, lb_ref, o_ref, full_ref, y_ref,
                 *, tt, taps):
    i = pl.program_id(1)
    ch = x_ref.shape[-1]
    sub = V7X_SUBLANES
    if halo_ref.shape[1] == CONV_HALO:
        head = jnp.where(i == 0, hist_ref[0], halo_ref[0])
    else:
        head = hist_ref[0]
    full_ref[0:CONV_HALO, :] = head
    full_ref[CONV_HALO:CONV_HALO + tt, :] = x_ref[0]
    first = CONV_HALO - (taps - 1)
    nv = tt // sub
    nf = (CONV_HALO + tt) // sub
    sub_idx = lax.broadcasted_iota(jnp.int32, (sub, V7X_LANES), 0)
    for c0 in range(0, ch, V7X_LANES):
        cols = slice(c0, c0 + V7X_LANES)
        f = [full_ref[j * sub:(j + 1) * sub, cols] for j in range(nf)]
        accs = [None] * nv
        for shift in range(sub):
            group = [(k, (first + k) // sub) for k in range(taps) if (first + k) % sub == shift]
            if not group:
                continue
            lo = min(a for _, a in group)
            hi = max(a for _, a in group) + nv
            if shift == 0:
                rows = {j: f[j] for j in range(lo, hi)}
            else:
                rolled = {j: pltpu.roll(f[j], sub - shift, 0) for j in range(lo, hi + 1)}
                keep = sub_idx < (sub - shift)
                rows = {j: jnp.where(keep, rolled[j], rolled[j + 1]) for j in range(lo, hi)}
            for k, a in group:
                wk = w_ref[k * sub:(k + 1) * sub, cols]
                for t in range(nv):
                    term = rows[a + t] * wk
                    accs[t] = term if accs[t] is None else accs[t] + term
        for t in range(nv):
            y_ref[t * sub:(t + 1) * sub, cols] = accs[t]
    y = _ln(y_ref[...] + b_ref[...], lg_ref[...], lb_ref[...])
    o_ref[0] = (_silu(y) * g_ref[0].astype(F32)).astype(o_ref.dtype)


def _conv_module(x, hist32, sgd, w_dw, b_dw, ln_g, ln_b, *, tt):
    batch, t, ch = x.shape
    taps = w_dw.shape[0]
    hpb = max(tt // CONV_HALO, 1)
    return pl.pallas_call(
        functools.partial(_conv_kernel, tt=tt, taps=taps),
        grid=(batch, t // tt),
        in_specs=[pl.BlockSpec((1, tt, ch), lambda b, i: (b, i, 0)),
                  pl.BlockSpec((1, min(CONV_HALO, t), ch), lambda b, i: (b, jnp.maximum(i * hpb - 1, 0), 0)),
                  pl.BlockSpec((1, CONV_HALO, ch), lambda b, i: (b, 0, 0)),
                  pl.BlockSpec((1, tt, ch), lambda b, i: (b, i, 0)),
                  pl.BlockSpec((taps * V7X_SUBLANES, ch), lambda b, i: (0, 0)),
                  pl.BlockSpec((1, ch), lambda b, i: (0, 0)),
                  pl.BlockSpec((1, ch), lambda b, i: (0, 0)),
                  pl.BlockSpec((1, ch), lambda b, i: (0, 0))],
        out_specs=pl.BlockSpec((1, tt, ch), lambda b, i: (b, i, 0)),
        out_shape=jax.ShapeDtypeStruct((batch, t, ch), BF16),
        scratch_shapes=[pltpu.VMEM((CONV_HALO + tt, ch), F32), pltpu.VMEM((tt, ch), F32)],
        compiler_params=_cparams(2),
        name="conv_module",
    )(x, x, hist32, sgd, jnp.repeat(w_dw, V7X_SUBLANES, axis=0), b_dw.reshape(1, ch),
      ln_g.reshape(1, ch), ln_b.reshape(1, ch))


def _outproj_kernel(a_ref, b_ref, y_ref, w_ref, gpost_ref, gnext_ref, ynew_ref, *h_refs):
    ka = a_ref.shape[1]
    acc = _dot(a_ref[...], w_ref[0:ka, :]) + _dot(b_ref[...], w_ref[ka:, :])
    yn = y_ref[...] + _rms(acc, gpost_ref[...])
    ynew_ref[...] = yn
    if h_refs:
        h_refs[0][...] = _rms(yn, gnext_ref[...]).astype(h_refs[0].dtype)


def _outproj(a, b, y, w_bf, layer, g_post, g_next, *, tm):
    m, d = y.shape
    ka, kb = a.shape[1], b.shape[1]
    emit_h = g_next is not None
    gn = g_next if emit_h else g_post
    out_specs = [pl.BlockSpec((tm, d), lambda i: (i, 0))]
    out_shape = [jax.ShapeDtypeStruct((m, d), F32)]
    if emit_h:
        out_specs.append(pl.BlockSpec((tm, d), lambda i: (i, 0)))
        out_shape.append(jax.ShapeDtypeStruct((m, d), BF16))
    outs = pl.pallas_call(
        _outproj_kernel,
        grid=(m // tm,),
        in_specs=[pl.BlockSpec((tm, ka), lambda i: (i, 0)),
                  pl.BlockSpec((tm, kb), lambda i: (i, 0)),
                  pl.BlockSpec((tm, d), lambda i: (i, 0)),
                  pl.BlockSpec((None, ka + kb, d), lambda i: (layer, 0, 0), pipeline_mode=pl.Buffered(1)),
                  pl.BlockSpec((1, d), lambda i: (0, 0)),
                  pl.BlockSpec((1, d), lambda i: (0, 0))],
        out_specs=out_specs,
        out_shape=out_shape,
        compiler_params=_cparams(1),
        name="out_proj",
    )(a, b, y, w_bf, g_post.reshape(1, d), gn.reshape(1, d))
    return (outs[0], outs[1]) if emit_h else (outs[0], None)


def _rope_table(pos, half):
    inv = jnp.power(jnp.float32(ROPE_THETA), -jnp.arange(half, dtype=jnp.float32) / half)
    ang = pos.astype(jnp.float32)[:, None] * inv[None, :]
    cos, sin = jnp.cos(ang), jnp.sin(ang)
    return jnp.concatenate([cos, cos, -sin, sin], axis=-1)


def _swap_halves(w):
    half = w.shape[-1] // 2
    return jnp.concatenate([w[..., half:], w[..., :half]], axis=-1)


def _even_weights(w_in_t, w_uq, w_uk, w_uv, dims):
    q_lora, kv_lora, qk_rope = dims
    kr0, half = q_lora + kv_lora, qk_rope // 2
    w_kr_ext = jnp.concatenate([w_in_t[kr0:kr0 + qk_rope], w_in_t[kr0 + half:kr0 + qk_rope],
                                w_in_t[kr0:kr0 + half]], axis=0)
    nope = w_uq.shape[-1] - qk_rope
    wq_ext = jnp.concatenate([w_uq, _swap_halves(w_uq[..., nope:])], axis=-1)
    wq_ext = wq_ext.reshape(q_lora, -1)
    return w_kr_ext, wq_ext, w_uk.reshape(kv_lora, -1), w_uv.reshape(kv_lora, -1)


def _even_layer(h, y, tab, batch, t, past_kv, pool_hist16, wts, layer, g_post, g_next, *, pos0, tm, sm_scale):
    (w_in, g_q, g_kv, w_kr_ext, wq_ext, w_uk2, w_uv2, w_pool, pool_scale, w_out_bf) = wts
    m = h.shape[0]
    q_lora, kv_lora = g_q.shape[0], g_kv.shape[0]
    mla_width = w_uv2.shape[1]
    pool_width = pool_scale.shape[0]
    nt = tab.shape[0] // tm
    rest0 = q_lora + kv_lora + 64

    cq = _simple_proj(h, w_in, layer, 0, q_lora, q_lora, _epi_rms(BF16), BF16, tm=tm, name="in_cq",
                      extras=[(g_q.reshape(1, q_lora), (1, q_lora), lambda j, i: (0, 0))], transposed=True)
    ckv, kr = _proj(h, [(w_in, layer, kv_lora, q_lora, True), (w_kr_ext, None, 128, 0, True)],
                    [(g_kv.reshape(1, kv_lora), (1, kv_lora), lambda j, i: (0, 0)),
                     (tab, (tm, 128), lambda j, i: (i % nt, 0))],
                    [((m, kv_lora), F32, (tm, kv_lora), lambda j, i: (i, 0)),
                     ((m, 64), F32, (tm, 64), lambda j, i: (i, 0))],
                    _epi_kv, tm=tm, nj=1, name="in_ckv")
    tn = 1024
    sga = _simple_proj(h, w_in, layer, rest0, mla_width, tn, _epi_silu, BF16, tm=tm, name="in_ga",
                       transposed=True)
    u_b = _simple_proj(h, w_in, layer, rest0 + mla_width, pool_width, tn, _epi_copy, F32, tm=tm, name="in_ub",
                       transposed=True)
    sgb = _simple_proj(h, w_in, layer, rest0 + mla_width + pool_width, pool_width, tn, _epi_silu, BF16,
                       tm=tm, name="in_gb", transposed=True)

    q = _qup(cq, wq_ext, tab, tm=min(tm, 512), sm_scale=sm_scale)
    if past_kv is None:
        k, vt = _kvup(ckv, kr, w_uk2, w_uv2.T, tm=min(tm, 512))
        a = _attn_prompt(q, k, vt, sga, batch=batch, seq=t, hb=8, tq=256)
    else:
        past_ckv, past_kr = past_kv
        assert past_ckv.shape[2] % CHUNK == 0 and t <= CHUNK
        qlat = _qlat(q, w_uk2)
        olat = _decode_attn(qlat, q, past_ckv, past_kr, layer, ckv, kr, ts=t, tk=min(1024, past_ckv.shape[2]))
        a = _olat(olat, w_uv2, sga)

    u3 = u_b.reshape(batch, t, pool_width)
    b = _pool_mix(u3, pool_hist16, sgb.reshape(batch, t, pool_width), w_pool, pool_scale,
                  tt=min(t, 256), pos0=pos0).reshape(m, pool_width)
    y_new, h_next = _outproj(a, b, y, w_out_bf, layer, g_post, g_next, tm=min(tm, 512))
    return y_new, h_next, ckv, kr, u3


def _odd_layer(h, y, batch, t, conv_hist32, wts, layer, g_post, g_next, *, tm, emit_vn):
    (w_in, ln_v_g, ln_v_b, w_sp, b_sp, w_dw, b_dw, ln_c_g, ln_c_b, w_out_bf) = wts
    m = h.shape[0]
    width = ln_v_g.shape[0]
    ch = ln_c_g.shape[0]
    tn = 1024
    uv = _simple_proj(h, w_in, layer, 0, 2 * width, tn, _epi_copy, BF16, tm=tm, name="in_uv")
    sgc = _simple_proj(h, w_in, layer, 2 * width, width, tn, _epi_silu, BF16, tm=tm, name="in_gc")
    tg = 512
    x_c = _proj(h, [(w_in, layer, tg, 3 * width, False), (w_in, layer, tg, 3 * width + ch, False)], [],
                [((m, ch), F32, (tm, tg), lambda j, i: (i, j))],
                _epi_glu, tm=tm, nj=ch // tg, name="in_glu")[0]
    sgd = _simple_proj(h, w_in, layer, 3 * width + 2 * ch, ch, tn, _epi_silu, BF16, tm=tm, name="in_gd")

    chunk = w_sp.shape[-1]
    gm = _gmlp(uv, sgc, w_sp, b_sp, ln_v_g, ln_v_b, tt=min(m, 512), chunk=chunk, emit_vn=emit_vn)
    c, vn = (gm[0], gm[1]) if emit_vn else (gm[0], None)
    x3 = x_c.reshape(batch, t, ch)
    yc = _conv_module(x3, conv_hist32, sgd.reshape(batch, t, ch), w_dw, b_dw, ln_c_g, ln_c_b,
                      tt=min(t, 128)).reshape(m, ch)
    y_new, h_next = _outproj(c, yc, y, w_out_bf, layer, g_post, g_next, tm=min(tm, 512))
    return y_new, h_next, x3, vn


def kernel(x_prompt, x_sample, cache_mla_ckv, cache_mla_krope, state_pool, state_conv, ln_pre, ln_post,
           w_in_even, g_q_lat, g_kv_lat, w_uq, w_uk, w_uv, w_pool, pool_scale, w_out_even, w_in_odd,
           ln_v_g, ln_v_b, w_sp, b_sp, w_dw, b_dw, ln_c_g, ln_c_b, w_out_odd):
    bp, sp, d = x_prompt.shape
    bs, ts, _ = x_sample.shape
    past = cache_mla_ckv.shape[2]
    depth = ln_pre.shape[0]
    q_lora, kv_lora = g_q_lat.shape[1], g_kv_lat.shape[1]
    qk_rope = cache_mla_krope.shape[-1]
    qk_nope = w_uq.shape[-1] - qk_rope
    pool_width = pool_scale.shape[1]
    pool_hist = state_pool.shape[2]
    conv_hist = state_conv.shape[2]
    ch = state_conv.shape[3]
    sm_scale = float((qk_nope + qk_rope) ** -0.5 * math.log2(math.e))
    mp, ms = bp * sp, bs * ts
    tmp, tms = min(1024, sp), ms

    tab_p = _rope_table(jnp.arange(sp), qk_rope // 2)
    tab_s = jnp.tile(_rope_table(past + jnp.arange(ts), qk_rope // 2), (bs, 1))

    yp = x_prompt.reshape(mp, d)
    ys = x_sample.reshape(ms, d)
    hp = _rms_cast(yp, ln_pre[0], tmp)
    hs = _rms_cast(ys, ln_pre[0], tms)

    w_out_even_bf = w_out_even.astype(BF16)
    w_out_odd_bf = w_out_odd.astype(BF16)
    w_in_even_t = jnp.swapaxes(w_in_even, 1, 2)
    cache_krope_t = jnp.swapaxes(cache_mla_krope, 2, 3)
    ckv_p, kr_p, pool_p, conv_p = [], [], [], []
    ckv_s, kr_s, pool_s, conv_s, v_s = [], [], [], [], []
    for layer in range(depth):
        i = layer // 2
        g_post = ln_post[layer]
        g_next = ln_pre[layer + 1] if layer + 1 < depth else None
        if layer % 2 == 0:
            w_kr_ext, wq_ext, w_uk2, w_uv2 = _even_weights(
                w_in_even_t[i], w_uq[i], w_uk[i], w_uv[i], (q_lora, kv_lora, qk_rope))
            wts = (w_in_even_t, g_q_lat[i], g_kv_lat[i], w_kr_ext, wq_ext, w_uk2, w_uv2,
                   w_pool[i], pool_scale[i], w_out_even_bf)
            zero_hist = jnp.zeros((bp, POOL_HALO, pool_width), F32)
            yp, hp, c1, k1, u1 = _even_layer(hp, yp, tab_p, bp, sp, None, zero_hist, wts, i, g_post, g_next,
                                             pos0=0, tm=tmp, sm_scale=sm_scale)
            hist16 = jnp.pad(state_pool[i], ((0, 0), (POOL_HALO - pool_hist, 0), (0, 0)))
            ys, hs, c2, k2, u2 = _even_layer(hs, ys, tab_s, bs, ts, (cache_mla_ckv, cache_krope_t),
                                             hist16, wts, i, g_post, g_next, pos0=past, tm=tms,
                                             sm_scale=sm_scale)
            ckv_p.append(c1.reshape(bp, sp, kv_lora))
            kr_p.append(k1.reshape(bp, sp, qk_rope))
            pool_p.append(u1[:, sp - pool_hist:])
            ckv_s.append(c2.reshape(bs, ts, kv_lora))
            kr_s.append(k2.reshape(bs, ts, qk_rope))
            pool_s.append(jnp.concatenate([state_pool[i], u2], axis=1)[:, -pool_hist:])
        else:
            wts = (w_in_odd, ln_v_g[i], ln_v_b[i], w_sp[i], b_sp[i][:, :, None], w_dw[i], b_dw[i],
                   ln_c_g[i], ln_c_b[i], w_out_odd_bf)
            zero_hist = jnp.zeros((bp, CONV_HALO, ch), F32)
            yp, hp, x1, _ = _odd_layer(hp, yp, bp, sp, zero_hist, wts, i, g_post, g_next, tm=tmp, emit_vn=False)
            lc = min(ts, w_sp.shape[-1])
            wts_s = wts[:3] + (w_sp[i][:, :lc, :lc], b_sp[i][:, :lc, None]) + wts[5:]
            hist32 = jnp.pad(state_conv[i], ((0, 0), (CONV_HALO - conv_hist, 0), (0, 0)))
            ys, hs, x2, v2 = _odd_layer(hs, ys, bs, ts, hist32, wts_s, i, g_post, g_next, tm=tms, emit_vn=True)
            conv_p.append(x1[:, sp - conv_hist:])
            conv_s.append(jnp.concatenate([state_conv[i], x2], axis=1)[:, -conv_hist:])
            v_s.append(v2.reshape(bs, ts, -1))

    return (yp.reshape(bp, sp, d), ys.reshape(bs, ts, d),
            jnp.stack(ckv_p), jnp.stack(kr_p), jnp.stack(pool_p), jnp.stack(conv_p),
            jnp.stack(ckv_s), jnp.stack(kr_s), jnp.stack(pool_s), jnp.stack(conv_s), jnp.stack(v_s))
```
